```python
import jax, jax.numpy as jnp
from jax import lax
import numpy as np

D_MODEL = 1024
BATCH = 8
SEQ = 2048
DEPTH = 4

MEM_LEN = 256
D_FF = ((8 * D_MODEL // 3 + 255) // 256) * 256
D_MIX = D_MODEL
W_A = 3 * D_MIX // 8
W_B = 3 * D_MIX // 8
W_C = D_MIX - W_A - W_B
RG_BLOCK = 64
RG_HEADS = W_A // RG_BLOCK
RG_CONV_K = 4
RG_C = 8.0
GLA_HEADS = 4
GLA_DV = W_B // GLA_HEADS
GLA_DK = GLA_DV // 2
GLA_RANK = 16
GLA_TAU = 16.0
GLA_CHUNK = 64
CONV_K = 31
CONV_GROUPS = 4
XA_HEADS = 4
XA_HEAD_DIM = D_MODEL // XA_HEADS
ALPHA = (2.0 * DEPTH) ** 0.25
BETA = (8.0 * DEPTH) ** -0.25
EPS = 1e-5
SPLIT_SIZES = (W_A, W_A, GLA_HEADS * GLA_DK, GLA_HEADS * GLA_DK, W_B, W_B, GLA_RANK, 2 * W_C)
N_IN_COLS = 2 * W_A + 2 * GLA_HEADS * GLA_DK + 2 * W_B + GLA_RANK + 2 * W_C

kernel_name = "hybrid_rglru_gla_conformer_deepnorm"


def layer_norm(x, g, b):
    xf = x.astype(jnp.float32)
    mu = jnp.mean(xf, axis=-1, keepdims=True)
    var = jnp.mean(jnp.square(xf - mu), axis=-1, keepdims=True)
    y = (xf - mu) * lax.rsqrt(var + EPS)
    return (y * g + b).astype(x.dtype)


def swiglu(x, w_in, w_out):
    gate, up = jnp.split(x @ w_in, 2, axis=-1)
    return (jax.nn.silu(gate) * up) @ w_out


def causal_dwconv(x, w, b):
    k, c = w.shape
    y = lax.conv_general_dilated(x, w[:, None, :].astype(x.dtype), window_strides=(1,),
                                 padding=[(k - 1, 0)],
                                 dimension_numbers=("NWC", "WIO", "NWC"),
                                 feature_group_count=c)
    return y + b


def split_cols(h):
    idx = []
    acc = 0
    for s in SPLIT_SIZES[:-1]:
        acc += s
        idx.append(acc)
    return jnp.split(h, idx, axis=-1)


def _lin_combine(left, right):
    a1, b1 = left
    a2, b2 = right
    return a1 * a2, a2 * b1 + b2


def rglru_group(xa, ya, conv_w, conv_b, w_r, b_r, w_i, b_i, lam):
    bsz, s, w = xa.shape
    xc = causal_dwconv(xa, conv_w, conv_b)
    xh = xc.reshape(bsz, s, RG_HEADS, RG_BLOCK)
    r = jax.nn.sigmoid(jnp.einsum("bshi,hij->bshj", xh, w_r).reshape(bsz, s, w) + b_r)
    i = jax.nn.sigmoid(jnp.einsum("bshi,hij->bshj", xh, w_i).reshape(bsz, s, w) + b_i)
    log_a = -RG_C * r.astype(jnp.float32) * jax.nn.softplus(-lam.astype(jnp.float32))
    a = jnp.exp(log_a)
    u = jnp.sqrt(-jnp.expm1(2.0 * log_a)) * (i * xc).astype(jnp.float32)
    _, h = lax.associative_scan(_lin_combine, (a, u), axis=1)
    return (h * jax.nn.gelu(ya.astype(jnp.float32))).astype(xa.dtype)


def gla_chunked(q, k, v, log_alpha):
    bsz, s, h, dk = q.shape
    dv = v.shape[-1]
    n = s // GLA_CHUNK
    q = (q * (dk ** -0.5)).reshape(bsz, n, GLA_CHUNK, h, dk)
    k = k.reshape(bsz, n, GLA_CHUNK, h, dk)
    v = v.reshape(bsz, n, GLA_CHUNK, h, dv)
    bcum = jnp.cumsum(log_alpha.reshape(bsz, n, GLA_CHUNK, h, dk), axis=2)
    b_last = bcum[:, :, -1:]
    q_dec = q * jnp.exp(bcum)
    k_inv = k * jnp.exp(-bcum)
    scores = jnp.einsum("bnihd,bnjhd->bnhij", q_dec, k_inv)
    causal = jnp.tril(jnp.ones((GLA_CHUNK, GLA_CHUNK), dtype=bool))
    scores = jnp.where(causal, scores, 0.0)
    o_intra = jnp.einsum("bnhij,bnjhe->bnihe", scores, v)
    k_tail = k * jnp.exp(b_last - bcum)
    u = jnp.einsum("bnchd,bnche->bnhde", k_tail, v)
    decay = jnp.exp(b_last[:, :, 0])

    def step(state, inp):
        dec, inc = inp
        return dec[..., None] * state + inc, state

    init = jnp.zeros((bsz, h, dk, dv), jnp.float32)
    _, s_prev = lax.scan(step, init, (jnp.moveaxis(decay, 1, 0), jnp.moveaxis(u, 1, 0)))
    s_prev = jnp.moveaxis(s_prev, 0, 1)
    o_inter = jnp.einsum("bnihd,bnhde->bnihe", q_dec, s_prev)
    return (o_intra + o_inter).reshape(bsz, s, h, dv)


def gla_group(q, k, v, g, lr, w_gate, b_gate, norm_g):
    bsz, s, _ = q.shape
    f32 = jnp.float32
    gate_pre = (lr @ w_gate + b_gate).astype(f32)
    log_alpha = (jax.nn.log_sigmoid(gate_pre) / GLA_TAU).reshape(bsz, s, GLA_HEADS, GLA_DK)
    o = gla_chunked(q.astype(f32).reshape(bsz, s, GLA_HEADS, GLA_DK),
                    k.astype(f32).reshape(bsz, s, GLA_HEADS, GLA_DK),
                    v.astype(f32).reshape(bsz, s, GLA_HEADS, GLA_DV), log_alpha)
    o = o * lax.rsqrt(jnp.mean(jnp.square(o), axis=-1, keepdims=True) + EPS) * norm_g
    o = o.reshape(bsz, s, W_B) * jax.nn.silu(g.astype(f32))
    return o.astype(q.dtype)


def conformer_conv_group(c, dw_w, dw_b, gn_g, gn_b):
    bsz, s, _ = c.shape
    a, gate = jnp.split(c, 2, axis=-1)
    u = causal_dwconv(a * jax.nn.sigmoid(gate), dw_w, dw_b)
    uf = u.astype(jnp.float32).reshape(bsz, s, CONV_GROUPS, W_C // CONV_GROUPS)
    mu = jnp.mean(uf, axis=-1, keepdims=True)
    var = jnp.mean(jnp.square(uf - mu), axis=-1, keepdims=True)
    un = ((uf - mu) * lax.rsqrt(var + EPS)).reshape(bsz, s, W_C) * gn_g + gn_b
    return jax.nn.silu(un).astype(c.dtype)


def hybrid_mixer(x, w_in, rg_conv_w, rg_conv_b, rg_w_r, rg_b_r, rg_w_i, rg_b_i, rg_lambda,
                 gla_w_gate, gla_b_gate, gla_norm_g, cv_dw_w, cv_dw_b, cv_gn_g, cv_gn_b, w_out):
    xa, ya, q, k, v, g, lr, c = split_cols(x @ w_in)
    out_a = rglru_group(xa, ya, rg_conv_w, rg_conv_b, rg_w_r, rg_b_r, rg_w_i, rg_b_i, rg_lambda)
    out_b = gla_group(q, k, v, g, lr, gla_w_gate, gla_b_gate, gla_norm_g)
    out_c = conformer_conv_group(c, cv_dw_w, cv_dw_b, cv_gn_g, cv_gn_b)
    return jnp.concatenate([out_a, out_b, out_c], axis=-1) @ w_out


def memory_cross_attn(x, mem, w_q, w_kv, w_o):
    bsz, s, _ = x.shape
    m = mem.shape[1]
    q = (x @ w_q).reshape(bsz, s, XA_HEADS, XA_HEAD_DIM)
    k, v = jnp.split(mem @ w_kv, 2, axis=-1)
    k = k.reshape(bsz, m, XA_HEADS, XA_HEAD_DIM)
    v = v.reshape(bsz, m, XA_HEADS, XA_HEAD_DIM)
    scores = jnp.einsum("bshd,bmhd->bhsm", q, k).astype(jnp.float32) * (XA_HEAD_DIM ** -0.5)
    p = jax.nn.softmax(scores, axis=-1).astype(x.dtype)
    o = jnp.einsum("bhsm,bmhd->bshd", p, v).reshape(bsz, s, D_MODEL)
    return o @ w_o


def setup_inputs(seed: int = 0) -> dict:
    key = jax.random.key(seed)
    ks = iter(jax.random.split(key, 64))

    def nrm(shape, scale):
        return jax.random.normal(next(ks), shape, jnp.float32) * scale

    def gain(shape):
        return 1.0 + nrm(shape, 0.02)

    L = DEPTH
    u = jax.random.uniform(next(ks), (L, W_A), jnp.float32, minval=0.9, maxval=0.999)
    s_a = u ** (1.0 / RG_C)
    rg_lambda = jnp.log(s_a) - jnp.log1p(-s_a)
    xa_w_kv = jnp.concatenate([nrm((L, D_MODEL, D_MODEL), D_MODEL ** -0.5),
                               nrm((L, D_MODEL, D_MODEL), BETA * D_MODEL ** -0.5)], axis=-1)
    return {
        "x": nrm((BATCH, SEQ, D_MODEL), 1.0),
        "mem": nrm((BATCH, MEM_LEN, D_MODEL), 1.0),
        "ffn1_w_in": nrm((L, D_MODEL, 2 * D_FF), D_MODEL ** -0.5),
        "ffn1_w_out": nrm((L, D_FF, D_MODEL), BETA * D_FF ** -0.5),
        "ln1_g": gain((L, D_MODEL)),
        "ln1_b": nrm((L, D_MODEL), 0.02),
        "mix_w_in": nrm((L, D_MODEL, N_IN_COLS), D_MODEL ** -0.5),
        "rg_conv_w": nrm((L, RG_CONV_K, W_A), RG_CONV_K ** -0.5),
        "rg_conv_b": nrm((L, W_A), 0.01),
        "rg_w_r": nrm((L, RG_HEADS, RG_BLOCK, RG_BLOCK), RG_BLOCK ** -0.5),
        "rg_b_r": nrm((L, W_A), 0.01),
        "rg_w_i": nrm((L, RG_HEADS, RG_BLOCK, RG_BLOCK), RG_BLOCK ** -0.5),
        "rg_b_i": nrm((L, W_A), 0.01),
        "rg_lambda": rg_lambda,
        "gla_w_gate": nrm((L, GLA_RANK, GLA_HEADS * GLA_DK), GLA_RANK ** -0.5),
        "gla_b_gate": nrm((L, GLA_HEADS * GLA_DK), 0.01),
        "gla_norm_g": gain((L, GLA_DV)),
        "cv_dw_w": nrm((L, CONV_K, W_C), CONV_K ** -0.5),
        "cv_dw_b": nrm((L, W_C), 0.01),
        "cv_gn_g": gain((L, W_C)),
        "cv_gn_b": nrm((L, W_C), 0.02),
        "mix_w_out": nrm((L, D_MIX, D_MODEL), BETA * D_MIX ** -0.5),
        "ln2_g": gain((L, D_MODEL)),
        "ln2_b": nrm((L, D_MODEL), 0.02),
        "xa_w_q": nrm((L, D_MODEL, D_MODEL), D_MODEL ** -0.5),
        "xa_w_kv": xa_w_kv,
        "xa_w_o": nrm((L, D_MODEL, D_MODEL), BETA * D_MODEL ** -0.5),
        "ln3_g": gain((L, D_MODEL)),
        "ln3_b": nrm((L, D_MODEL), 0.02),
        "ffn2_w_in": nrm((L, D_MODEL, 2 * D_FF), D_MODEL ** -0.5),
        "ffn2_w_out": nrm((L, D_FF, D_MODEL), BETA * D_FF ** -0.5),
        "ln4_g": gain((L, D_MODEL)),
        "ln4_b": nrm((L, D_MODEL), 0.02),
    }


def reference(x, mem, ffn1_w_in, ffn1_w_out, ln1_g, ln1_b, mix_w_in, rg_conv_w, rg_conv_b,
              rg_w_r, rg_b_r, rg_w_i, rg_b_i, rg_lambda, gla_w_gate, gla_b_gate, gla_norm_g,
              cv_dw_w, cv_dw_b, cv_gn_g, cv_gn_b, mix_w_out, ln2_g, ln2_b, xa_w_q, xa_w_kv,
              xa_w_o, ln3_g, ln3_b, ffn2_w_in, ffn2_w_out, ln4_g, ln4_b):
    for l in range(DEPTH):
        x = layer_norm(ALPHA * x + 0.5 * swiglu(x, ffn1_w_in[l], ffn1_w_out[l]), ln1_g[l], ln1_b[l])
        mix = hybrid_mixer(x, mix_w_in[l], rg_conv_w[l], rg_conv_b[l], rg_w_r[l], rg_b_r[l],
                           rg_w_i[l], rg_b_i[l], rg_lambda[l], gla_w_gate[l], gla_b_gate[l],
                           gla_norm_g[l], cv_dw_w[l], cv_dw_b[l], cv_gn_g[l], cv_gn_b[l],
                           mix_w_out[l])
        x = layer_norm(ALPHA * x + mix, ln2_g[l], ln2_b[l])
        x = layer_norm(ALPHA * x + memory_cross_attn(x, mem, xa_w_q[l], xa_w_kv[l], xa_w_o[l]),
                       ln3_g[l], ln3_b[l])
        x = layer_norm(ALPHA * x + 0.5 * swiglu(x, ffn2_w_in[l], ffn2_w_out[l]), ln4_g[l], ln4_b[l])
    return x
```

```python
import functools

import numpy as np
import jax
import jax.numpy as jnp
from jax import lax
from jax.experimental import pallas as pl
from jax.experimental.pallas import tpu as pltpu

D_MODEL = 1024
BATCH = 8
SEQ = 2048
DEPTH = 4
MEM_LEN = 256
D_FF = 2816
W_A = 384
W_B = 384
W_C = 256
RG_BLOCK = 64
RG_HEADS = W_A // RG_BLOCK
RG_CONV_K = 4
RG_C = 8.0
GLA_HEADS = 4
GLA_DV = W_B // GLA_HEADS
GLA_DK = GLA_DV // 2
GLA_RANK = 16
GLA_TAU = 16.0
GLA_CHUNK = 64
CONV_K = 31
CONV_GROUPS = 4
XA_HEADS = 4
XA_HEAD_DIM = D_MODEL // XA_HEADS
ALPHA = (2.0 * DEPTH) ** 0.25
EPS = 1e-5

BF16 = jnp.bfloat16
F32 = jnp.float32

QK_PAD = 256
LR_PAD = 128
COL_RG = 0
COL_GLA = 2 * W_A
N_GLA = 2 * QK_PAD + 2 * W_B + LR_PAD
COL_CV = COL_GLA + N_GLA
N_IN_PAD = COL_CV + 2 * W_C

FFN_TM = 512
FFN_FC = 256
MIX_TM = 256
XA_TM = 512
RG_HIST = 8
CV_HIST = 32
VMEM_LIMIT = 56 * 1024 * 1024


def _dot(a, b):
    return jnp.dot(a, b, preferred_element_type=F32)


def _dot_nt(a, b):
    return lax.dot_general(a, b, (((1,), (1,)), ((), ())), preferred_element_type=F32)


def _dot_tn(a, b):
    return lax.dot_general(a, b, (((0,), (0,)), ((), ())), preferred_element_type=F32)


def _sigmoid(x):
    return 1.0 / (1.0 + jnp.exp(-x))


def _softplus(x):
    return jnp.maximum(x, 0.0) + jnp.log1p(jnp.exp(-jnp.abs(x)))


def _neg_expm1_of_2log(a, log_a):
    x2 = 2.0 * log_a
    sq = a * a
    sqm1 = sq - 1.0
    em1 = jnp.where(sq == 1.0, x2, sqm1 * x2 / jnp.log(sq))
    return -jnp.where(sqm1 == -1.0, -1.0, em1)


def _layer_norm(z, g, b):
    mu = jnp.mean(z, axis=-1, keepdims=True)
    d = z - mu
    var = jnp.mean(d * d, axis=-1, keepdims=True)
    return d * lax.rsqrt(var + EPS) * g + b


def _group_sum(y, ones_bf16):
    hi = y.astype(BF16)
    lo = (y - hi.astype(F32)).astype(BF16)
    return _dot(hi, ones_bf16) + _dot(lo, ones_bf16)


def _ffn_kernel(x_ref, win_ref, wout_ref, g_ref, b_ref, o_ref, act_ref):
    x = x_ref[...]
    xb = x.astype(BF16)
    for c in range(D_FF // FFN_FC):
        gate = _dot(xb, win_ref[:, c * FFN_FC:(c + 1) * FFN_FC])
        up = _dot(xb, win_ref[:, D_FF + c * FFN_FC:D_FF + (c + 1) * FFN_FC])
        act_ref[:, c * FFN_FC:(c + 1) * FFN_FC] = (gate * _sigmoid(gate) * up).astype(BF16)
    y = _dot(act_ref[...], wout_ref[...])
    o_ref[...] = _layer_norm(ALPHA * x + 0.5 * y, g_ref[...], b_ref[...])


def _const_spec(shape, layer):
    nd = len(shape)
    return pl.BlockSpec((None,) + tuple(shape), lambda *_: (layer,) + (0,) * nd,
                        pipeline_mode=pl.Buffered(1))


def _ffn(x2d, w_in, w_out, g, b, layer):
    t = x2d.shape[0]
    return pl.pallas_call(
        _ffn_kernel,
        grid=(t // FFN_TM,),
        in_specs=[
            pl.BlockSpec((FFN_TM, D_MODEL), lambda i: (i, 0)),
            _const_spec((D_MODEL, 2 * D_FF), layer),
            _const_spec((D_FF, D_MODEL), layer),
            _const_spec((1, D_MODEL), layer),
            _const_spec((1, D_MODEL), layer),
        ],
        out_specs=pl.BlockSpec((FFN_TM, D_MODEL), lambda i: (i, 0)),
        out_shape=jax.ShapeDtypeStruct((t, D_MODEL), F32),
        scratch_shapes=[pltpu.VMEM((FFN_TM, D_FF), BF16)],
        compiler_params=pltpu.CompilerParams(
            dimension_semantics=("arbitrary",), vmem_limit_bytes=VMEM_LIMIT),
        name="ffn_ln",
    )(x2d, w_in, w_out, g, b)


def _mixer_kernel(x_ref, win_ref, rgcw_ref, rgcb_ref, wri_ref, bri_ref, lam_ref,
                  wgate_ref, bgate_ref, ng_ref, hones_ref,
                  cvw_ref, cvb_ref, gng_ref, gnb_ref, gones_ref,
                  wout_ref, lng_ref, lnb_ref,
                  o_ref,
                  mix_ref, exta_ref, extc_ref, h_ref, st_ref, obuf_ref):
    tm = MIX_TM

    @pl.when(pl.program_id(1) == 0)
    def _():
        exta_ref[0:RG_HIST, :] = jnp.zeros((RG_HIST, W_A), F32)
        extc_ref[0:CV_HIST, :] = jnp.zeros((CV_HIST, W_C), F32)
        h_ref[...] = jnp.zeros_like(h_ref)
        st_ref[...] = jnp.zeros_like(st_ref)

    x = x_ref[...]
    xb = x.astype(BF16)

    xy = _dot(xb, win_ref[:, COL_RG:COL_RG + 2 * W_A])
    ya = xy[:, W_A:]
    exta_ref[RG_HIST:RG_HIST + tm, :] = xy[:, :W_A]
    xc = rgcb_ref[...]
    for k in range(RG_CONV_K):
        off = RG_HIST - (RG_CONV_K - 1) + k
        xc = xc + rgcw_ref[k:k + 1, :] * exta_ref[off:off + tm, :]
    exta_ref[0:RG_HIST, :] = exta_ref[tm:tm + RG_HIST, :]

    ri = _dot(xc.astype(BF16), wri_ref[...]) + bri_ref[...]
    r = _sigmoid(ri[:, :W_A])
    ig = _sigmoid(ri[:, W_A:])
    log_a = (-RG_C * r) * _softplus(-lam_ref[...])
    a = jnp.exp(log_a)
    u = jnp.sqrt(_neg_expm1_of_2log(a, log_a)) * (ig * xc)
    row = lax.broadcasted_iota(jnp.int32, (tm, W_A), 0)
    d = 1
    while d < tm:
        keep = row >= d
        a_s = jnp.where(keep, pltpu.roll(a, d, 0), 1.0)
        u_s = jnp.where(keep, pltpu.roll(u, d, 0), 0.0)
        u = a * u_s + u
        a = a * a_s
        d *= 2
    h = u + a * h_ref[0:1, :]
    h_ref[...] = jnp.broadcast_to(h[tm - 1:tm, :], h_ref.shape)
    gelu = 0.5 * ya * (1.0 + jnp.tanh(0.7978845608028654 * (ya + 0.044715 * (ya * ya * ya))))
    mix_ref[:, 0:W_A] = (h * gelu).astype(BF16)

    gl = _dot(xb, win_ref[:, COL_GLA:COL_GLA + N_GLA])
    q = gl[:, 0:QK_PAD]
    kk = gl[:, QK_PAD:2 * QK_PAD]
    v = gl[:, 2 * QK_PAD:2 * QK_PAD + W_B]
    g = gl[:, 2 * QK_PAD + W_B:2 * QK_PAD + 2 * W_B]
    lr = gl[:, 2 * QK_PAD + 2 * W_B:]
    gate_pre = _dot(lr.astype(BF16), wgate_ref[...]) + bgate_ref[...]
    bc = -_softplus(-gate_pre) * (1.0 / GLA_TAU)
    rowc = lax.broadcasted_iota(jnp.int32, (tm, QK_PAD), 0) & (GLA_CHUNK - 1)
    d = 1
    while d < GLA_CHUNK:
        bc = bc + jnp.where(rowc >= d, pltpu.roll(bc, d, 0), 0.0)
        d *= 2

    vb = v.astype(BF16)
    nrow = GLA_HEADS * GLA_CHUNK
    lane_q = lax.broadcasted_iota(jnp.int32, (GLA_CHUNK, QK_PAD), 1)
    lane_v = lax.broadcasted_iota(jnp.int32, (GLA_CHUNK, W_B), 1)
    causal = ((lax.broadcasted_iota(jnp.int32, (nrow, GLA_CHUNK), 0) & (GLA_CHUNK - 1))
              >= lax.broadcasted_iota(jnp.int32, (nrow, GLA_CHUNK), 1))
    st_row = lax.broadcasted_iota(jnp.int32, (W_B, QK_PAD), 0)
    st_lane = lax.broadcasted_iota(jnp.int32, (W_B, QK_PAD), 1)
    blockdiag = None
    for hh in range(GLA_HEADS):
        m = ((st_row >= hh * GLA_DV) & (st_row < (hh + 1) * GLA_DV)
             & (st_lane >= hh * GLA_DK) & (st_lane < (hh + 1) * GLA_DK))
        blockdiag = m if blockdiag is None else (blockdiag | m)

    for c in range(tm // GLA_CHUNK):
        r0 = c * GLA_CHUNK
        bcc = bc[r0:r0 + GLA_CHUNK, :]
        blast = bcc[GLA_CHUNK - 1:GLA_CHUNK, :]
        qd = (q[r0:r0 + GLA_CHUNK, :] * (GLA_DK ** -0.5)) * jnp.exp(bcc)
        kc = kk[r0:r0 + GLA_CHUNK, :]
        ki = (kc * jnp.exp(-bcc)).astype(BF16)
        kt = (kc * jnp.exp(blast - bcc)).astype(BF16)
        decay = jnp.exp(blast)
        vc = vb[r0:r0 + GLA_CHUNK, :]
        qbd = jnp.concatenate(
            [jnp.where((lane_q >= hh * GLA_DK) & (lane_q < (hh + 1) * GLA_DK), qd, 0.0)
             for hh in range(GLA_HEADS)], axis=0).astype(BF16)
        s = _dot_nt(qbd, ki)
        p = jnp.where(causal, s, 0.0).astype(BF16)
        oi = _dot(p, vc)
        o_c = None
        for hh in range(GLA_HEADS):
            part = jnp.where((lane_v >= hh * GLA_DV) & (lane_v < (hh + 1) * GLA_DV),
                             oi[hh * GLA_CHUNK:(hh + 1) * GLA_CHUNK, :], 0.0)
            o_c = part if o_c is None else o_c + part
        st = st_ref[...]
        o_c = o_c + _dot_nt(qd.astype(BF16), st.astype(BF16))
        obuf_ref[r0:r0 + GLA_CHUNK, :] = o_c
        inc = _dot_tn(vc, kt)
        st_ref[...] = st * decay + jnp.where(blockdiag, inc, 0.0)

    o = obuf_ref[...]
    ms = _group_sum(o * o, hones_ref[...]) * (1.0 / GLA_DV)
    o = o * lax.rsqrt(ms + EPS) * ng_ref[...]
    mix_ref[:, W_A:W_A + W_B] = (o * (g * _sigmoid(g))).astype(BF16)

    cc = _dot(xb, win_ref[:, COL_CV:COL_CV + 2 * W_C])
    extc_ref[CV_HIST:CV_HIST + tm, :] = cc[:, :W_C] * _sigmoid(cc[:, W_C:])
    acc = cvb_ref[...]
    for k in range(CONV_K):
        off = CV_HIST - (CONV_K - 1) + k
        acc = acc + cvw_ref[k:k + 1, :] * extc_ref[off:off + tm, :]
    extc_ref[0:CV_HIST, :] = extc_ref[tm:tm + CV_HIST, :]
    gsz = W_C // CONV_GROUPS
    mu = _group_sum(acc, gones_ref[...]) * (1.0 / gsz)
    dd = acc - mu
    var = _group_sum(dd * dd, gones_ref[...]) * (1.0 / gsz)
    un = dd * lax.rsqrt(var + EPS) * gng_ref[...] + gnb_ref[...]
    mix_ref[:, W_A + W_B:] = (un * _sigmoid(un)).astype(BF16)

    y = _dot(mix_ref[...], wout_ref[...])
    o_ref[...] = _layer_norm(ALPHA * x + y, lng_ref[...], lnb_ref[...])


def _shared_spec(shape):
    nd = len(shape)
    return pl.BlockSpec(tuple(shape), lambda *_: (0,) * nd, pipeline_mode=pl.Buffered(1))


def _mixer(x3d, p, layer):
    bsz, s, _ = x3d.shape
    tm = MIX_TM
    xspec = pl.BlockSpec((None, tm, D_MODEL), lambda b, t: (b, t, 0))
    return pl.pallas_call(
        _mixer_kernel,
        grid=(bsz, s // tm),
        in_specs=[
            xspec,
            _const_spec((D_MODEL, N_IN_PAD), layer),
            _const_spec((RG_CONV_K, W_A), layer),
            _const_spec((1, W_A), layer),
            _const_spec((W_A, 2 * W_A), layer),
            _const_spec((1, 2 * W_A), layer),
            _const_spec((1, W_A), layer),
            _const_spec((LR_PAD, QK_PAD), layer),
            _const_spec((1, QK_PAD), layer),
            _const_spec((1, W_B), layer),
            _shared_spec((W_B, W_B)),
            _const_spec((CONV_K, W_C), layer),
            _const_spec((1, W_C), layer),
            _const_spec((1, W_C), layer),
            _const_spec((1, W_C), layer),
            _shared_spec((W_C, W_C)),
            _const_spec((D_MODEL, D_MODEL), layer),
            _const_spec((1, D_MODEL), layer),
            _const_spec((1, D_MODEL), layer),
        ],
        out_specs=xspec,
        out_shape=jax.ShapeDtypeStruct(x3d.shape, F32),
        scratch_shapes=[
            pltpu.VMEM((tm, D_MODEL), BF16),
            pltpu.VMEM((RG_HIST + tm, W_A), F32),
            pltpu.VMEM((CV_HIST + tm, W_C), F32),
            pltpu.VMEM((8, W_A), F32),
            pltpu.VMEM((W_B, QK_PAD), F32),
            pltpu.VMEM((tm, W_B), F32),
        ],
        compiler_params=pltpu.CompilerParams(
            dimension_semantics=("arbitrary", "arbitrary"), vmem_limit_bytes=VMEM_LIMIT),
        name="mixer_ln",
    )(x3d, p["mix_w_in"], p["rg_conv_w"], p["rg_conv_b"], p["rg_w_ri"], p["rg_b_ri"],
      p["rg_lambda"], p["gla_w_gate"], p["gla_b_gate"], p["gla_norm_g"], p["head_ones"],
      p["cv_dw_w"], p["cv_dw_b"], p["cv_gn_g"], p["cv_gn_b"], p["group_ones"],
      p["mix_w_out"], p["ln2_g"], p["ln2_b"])


def _kv_kernel(m_ref, w_ref, k_ref, v_ref):
    kv = _dot(m_ref[...].astype(BF16), w_ref[...])
    k_ref[...] = kv[:, :D_MODEL].astype(BF16)
    v_ref[...] = kv[:, D_MODEL:].astype(BF16)


def _kv_proj(mem3d, w_kv, layer):
    bsz, m, _ = mem3d.shape
    mspec = pl.BlockSpec((None, m, D_MODEL), lambda b: (b, 0, 0))
    return pl.pallas_call(
        _kv_kernel,
        grid=(bsz,),
        in_specs=[mspec, _const_spec((D_MODEL, 2 * D_MODEL), layer)],
        out_specs=[mspec, mspec],
        out_shape=[jax.ShapeDtypeStruct(mem3d.shape, BF16)] * 2,
        compiler_params=pltpu.CompilerParams(
            dimension_semantics=("arbitrary",), vmem_limit_bytes=VMEM_LIMIT),
        name="xattn_kv",
    )(mem3d, w_kv)


def _xattn_kernel(x_ref, k_ref, v_ref, wq_ref, wo_ref, g_ref, b_ref, o_ref, att_ref):
    x = x_ref[...]
    q = _dot(x.astype(BF16), wq_ref[...]).astype(BF16)
    for hh in range(XA_HEADS):
        c0 = hh * XA_HEAD_DIM
        s = _dot_nt(q[:, c0:c0 + XA_HEAD_DIM], k_ref[:, c0:c0 + XA_HEAD_DIM]) * (XA_HEAD_DIM ** -0.5)
        e = jnp.exp(s - jnp.max(s, axis=-1, keepdims=True))
        p = e / jnp.sum(e, axis=-1, keepdims=True)
        att_ref[:, c0:c0 + XA_HEAD_DIM] = _dot(p.astype(BF16), v_ref[:, c0:c0 + XA_HEAD_DIM]).astype(BF16)
    y = _dot(att_ref[...], wo_ref[...])
    o_ref[...] = _layer_norm(ALPHA * x + y, g_ref[...], b_ref[...])


def _xattn(x3d, k3d, v3d, w_q, w_o, g, b, layer):
    bsz, s, _ = x3d.shape
    m = k3d.shape[1]
    tm = XA_TM
    xspec = pl.BlockSpec((None, tm, D_MODEL), lambda bi, t: (bi, t, 0))
    mspec = pl.BlockSpec((None, m, D_MODEL), lambda bi, t: (bi, 0, 0))
    return pl.pallas_call(
        _xattn_kernel,
        grid=(bsz, s // tm),
        in_specs=[
            xspec, mspec, mspec,
            _const_spec((D_MODEL, D_MODEL), layer),
            _const_spec((D_MODEL, D_MODEL), layer),
            _const_spec((1, D_MODEL), layer),
            _const_spec((1, D_MODEL), layer),
        ],
        out_specs=xspec,
        out_shape=jax.ShapeDtypeStruct(x3d.shape, F32),
        scratch_shapes=[pltpu.VMEM((tm, D_MODEL), BF16)],
        compiler_params=pltpu.CompilerParams(
            dimension_semantics=("arbitrary", "arbitrary"), vmem_limit_bytes=VMEM_LIMIT),
        name="xattn_ln",
    )(x3d, k3d, v3d, w_q, w_o, g, b)


def _block_ones(n, group):
    idx = np.arange(n) // group
    return jnp.asarray((idx[:, None] == idx[None, :]).astype(np.float32), dtype=BF16)


def _prepare_params(mix_w_in, rg_conv_b, rg_w_r, rg_b_r, rg_w_i, rg_b_i, rg_lambda,
                    gla_w_gate, gla_b_gate, gla_norm_g, cv_dw_b, cv_gn_g, cv_gn_b):
    L = DEPTH
    nqk = GLA_HEADS * GLA_DK
    c_q = 2 * W_A
    c_k = c_q + nqk
    c_v = c_k + nqk
    c_g = c_v + W_B
    c_lr = c_g + W_B
    c_cv = c_lr + GLA_RANK

    def zeros(n):
        return jnp.zeros((L, D_MODEL, n), mix_w_in.dtype)

    w_in = jnp.concatenate([
        mix_w_in[:, :, :c_q],
        mix_w_in[:, :, c_q:c_k], zeros(QK_PAD - nqk),
        mix_w_in[:, :, c_k:c_v], zeros(QK_PAD - nqk),
        mix_w_in[:, :, c_v:c_lr],
        mix_w_in[:, :, c_lr:c_cv], zeros(LR_PAD - GLA_RANK),
        mix_w_in[:, :, c_cv:],
    ], axis=-1).astype(BF16)

    eye = jnp.eye(RG_HEADS, dtype=rg_w_r.dtype)

    def blockdiag(w):
        return jnp.einsum("lhij,hg->lhigj", w, eye).reshape(L, W_A, W_A)

    w_ri = jnp.concatenate([blockdiag(rg_w_r), blockdiag(rg_w_i)], axis=-1).astype(BF16)
    b_ri = jnp.concatenate([rg_b_r, rg_b_i], axis=-1)[:, None, :]
    w_gate = jnp.pad(gla_w_gate, ((0, 0), (0, LR_PAD - GLA_RANK), (0, QK_PAD - nqk))).astype(BF16)
    b_gate = jnp.pad(gla_b_gate, ((0, 0), (0, QK_PAD - nqk)))[:, None, :]
    return {
        "mix_w_in": w_in,
        "rg_conv_b": rg_conv_b[:, None, :],
        "rg_w_ri": w_ri,
        "rg_b_ri": b_ri,
        "rg_lambda": rg_lambda[:, None, :],
        "gla_w_gate": w_gate,
        "gla_b_gate": b_gate,
        "gla_norm_g": jnp.tile(gla_norm_g, (1, GLA_HEADS))[:, None, :],
        "head_ones": _block_ones(W_B, GLA_DV),
        "cv_dw_b": cv_dw_b[:, None, :],
        "cv_gn_g": cv_gn_g[:, None, :],
        "cv_gn_b": cv_gn_b[:, None, :],
        "group_ones": _block_ones(W_C, W_C // CONV_GROUPS),
    }


def kernel(x, mem, ffn1_w_in, ffn1_w_out, ln1_g, ln1_b, mix_w_in, rg_conv_w, rg_conv_b, rg_w_r, rg_b_r, rg_w_i, rg_b_i, rg_lambda, gla_w_gate, gla_b_gate, gla_norm_g, cv_dw_w, cv_dw_b, cv_gn_g, cv_gn_b, mix_w_out, ln2_g, ln2_b, xa_w_q, xa_w_kv, xa_w_o, ln3_g, ln3_b, ffn2_w_in, ffn2_w_out, ln4_g, ln4_b):
    bsz, s, dm = x.shape
    p = _prepare_params(mix_w_in, rg_conv_b, rg_w_r, rg_b_r, rg_w_i, rg_b_i, rg_lambda,
                        gla_w_gate, gla_b_gate, gla_norm_g, cv_dw_b, cv_gn_g, cv_gn_b)
    p.update({
        "rg_conv_w": rg_conv_w, "cv_dw_w": cv_dw_w,
        "mix_w_out": mix_w_out.astype(BF16),
        "ln2_g": ln2_g[:, None, :], "ln2_b": ln2_b[:, None, :],
    })
    f1_in, f1_out = ffn1_w_in.astype(BF16), ffn1_w_out.astype(BF16)
    f2_in, f2_out = ffn2_w_in.astype(BF16), ffn2_w_out.astype(BF16)
    w_q, w_kv, w_o = xa_w_q.astype(BF16), xa_w_kv.astype(BF16), xa_w_o.astype(BF16)
    ln1g, ln1b = ln1_g[:, None, :], ln1_b[:, None, :]
    ln3g, ln3b = ln3_g[:, None, :], ln3_b[:, None, :]
    ln4g, ln4b = ln4_g[:, None, :], ln4_b[:, None, :]

    for l in range(DEPTH):
        x = _ffn(x.reshape(bsz * s, dm), f1_in, f1_out, ln1g, ln1b, l).reshape(bsz, s, dm)
        x = _mixer(x, p, l)
        k3d, v3d = _kv_proj(mem, w_kv, l)
        x = _xattn(x, k3d, v3d, w_q, w_o, ln3g, ln3b, l)
        x = _ffn(x.reshape(bsz * s, dm), f2_in, f2_out, ln4g, ln4b, l).reshape(bsz, s, dm)
    return x
```

```python
import numpy as np
import jax
import jax.numpy as jnp
from jax import lax
from jax.experimental import pallas as pl
from jax.experimental.pallas import tpu as pltpu

D_MODEL = 1024
BATCH = 8
SEQ = 2048
DEPTH = 4
MEM_LEN = 256
D_FF = 2816
W_A = 384
W_B = 384
W_C = 256
RG_BLOCK = 64
RG_HEADS = W_A // RG_BLOCK
RG_CONV_K = 4
RG_C = 8.0
GLA_HEADS = 4
GLA_DV = W_B // GLA_HEADS
GLA_DK = GLA_DV // 2
GLA_RANK = 16
GLA_TAU = 16.0
GLA_CHUNK = 64
CONV_K = 31
CONV_GROUPS = 4
XA_HEADS = 4
XA_HEAD_DIM = D_MODEL // XA_HEADS
ALPHA = (2.0 * DEPTH) ** 0.25
EPS = 1e-5

BF16 = jnp.bfloat16
F32 = jnp.float32

SUBLANES = 8

QK_PAD = 256
LR_PAD = 128
COL_RG = 0
COL_GLA = 2 * W_A
N_GLA = 2 * QK_PAD + 2 * W_B + LR_PAD
COL_CV = COL_GLA + N_GLA
N_IN_PAD = COL_CV + 2 * W_C

FFN_TM = 512
FFN_FC = 256
MIX_TM = 256
MIX_G = MIX_TM // SUBLANES
MIX_NCH = MIX_TM // GLA_CHUNK
SEG_PER_CHUNK = GLA_CHUNK // MIX_G
XA_TM = 512
VMEM_LIMIT = 56 * 1024 * 1024

assert GLA_CHUNK % MIX_G == 0 and CONV_K - 1 <= MIX_G


def _dot(a, b):
    return jnp.dot(a, b, preferred_element_type=F32)


def _dot_nt(a, b):
    return lax.dot_general(a, b, (((1,), (1,)), ((), ())), preferred_element_type=F32)


def _dot_tn(a, b):
    return lax.dot_general(a, b, (((0,), (0,)), ((), ())), preferred_element_type=F32)


def _sigmoid(x):
    return 1.0 / (1.0 + jnp.exp(-x))


def _softplus(x):
    return jnp.maximum(x, 0.0) + jnp.log1p(jnp.exp(-jnp.abs(x)))


def _neg_expm1_of_2log(a, log_a):
    x2 = 2.0 * log_a
    sq = a * a
    sqm1 = sq - 1.0
    em1 = jnp.where(sq == 1.0, x2, sqm1 * x2 / jnp.log(sq))
    return -jnp.where(sqm1 == -1.0, -1.0, em1)


def _layer_norm(z, g, b):
    mu = jnp.mean(z, axis=-1, keepdims=True)
    d = z - mu
    var = jnp.mean(d * d, axis=-1, keepdims=True)
    return d * lax.rsqrt(var + EPS) * g + b


def _group_sum(y, ones_bf16):
    hi = y.astype(BF16)
    lo = (y - hi.astype(F32)).astype(BF16)
    return _dot(hi, ones_bf16) + _dot(lo, ones_bf16)


def _rows(v, g, n=1):
    return v[g * SUBLANES:(g + n) * SUBLANES, :]


def _ffn_kernel(x_ref, win_ref, wout_ref, g_ref, b_ref, o_ref, act_ref):
    x = x_ref[...]
    xb = x.astype(BF16)
    for c in range(D_FF // FFN_FC):
        gate = _dot(xb, win_ref[:, c * FFN_FC:(c + 1) * FFN_FC])
        up = _dot(xb, win_ref[:, D_FF + c * FFN_FC:D_FF + (c + 1) * FFN_FC])
        act_ref[:, c * FFN_FC:(c + 1) * FFN_FC] = (gate * _sigmoid(gate) * up).astype(BF16)
    y = _dot(act_ref[...], wout_ref[...])
    o_ref[...] = _layer_norm(ALPHA * x + 0.5 * y, g_ref[...], b_ref[...])


def _const_spec(shape, layer):
    nd = len(shape)
    return pl.BlockSpec((None,) + tuple(shape), lambda *_: (layer,) + (0,) * nd,
                        pipeline_mode=pl.Buffered(1))


def _shared_spec(shape):
    nd = len(shape)
    return pl.BlockSpec(tuple(shape), lambda *_: (0,) * nd, pipeline_mode=pl.Buffered(1))


def _ffn(x2d, w_in, w_out, g, b, layer):
    t = x2d.shape[0]
    return pl.pallas_call(
        _ffn_kernel,
        grid=(t // FFN_TM,),
        in_specs=[
            pl.BlockSpec((FFN_TM, D_MODEL), lambda i: (i, 0)),
            _const_spec((D_MODEL, 2 * D_FF), layer),
            _const_spec((D_FF, D_MODEL), layer),
            _const_spec((1, D_MODEL), layer),
            _const_spec((1, D_MODEL), layer),
        ],
        out_specs=pl.BlockSpec((FFN_TM, D_MODEL), lambda i: (i, 0)),
        out_shape=jax.ShapeDtypeStruct((t, D_MODEL), F32),
        scratch_shapes=[pltpu.VMEM((FFN_TM, D_FF), BF16)],
        compiler_params=pltpu.CompilerParams(
            dimension_semantics=("arbitrary",), vmem_limit_bytes=VMEM_LIMIT),
        name="ffn_ln",
    )(x2d, w_in, w_out, g, b)


def _shifted_history(cur, hist_ref, n_hist):
    tail = _rows(cur, MIX_G - n_hist, n_hist)
    sub = lax.broadcasted_iota(jnp.int32, tail.shape, 0) & (SUBLANES - 1)
    merged = jnp.where(sub == SUBLANES - 1, hist_ref[...], tail)
    hist_ref[...] = tail
    return [pltpu.roll(_rows(merged, i), 1, 0) for i in range(n_hist)]


def _causal_conv(cur, hist_ref, w_ref, b_ref, k_taps):
    ext = jnp.concatenate(_shifted_history(cur, hist_ref, k_taps - 1) + [cur], axis=0)
    acc = b_ref[...]
    for k in range(k_taps):
        acc = acc + w_ref[k:k + 1, :] * _rows(ext, k, MIX_G)
    return acc


def _linear_recurrence(a, u, h_ref):
    hs, ps = [_rows(u, 0)], [_rows(a, 0)]
    for g in range(1, MIX_G):
        ag = _rows(a, g)
        hs.append(ag * hs[-1] + _rows(u, g))
        ps.append(ag * ps[-1])
    hc, pc = hs[-1], ps[-1]
    sub = lax.broadcasted_iota(jnp.int32, hc.shape, 0)
    d = 1
    while d < SUBLANES:
        keep = sub >= d
        h_s = jnp.where(keep, pltpu.roll(hc, d, 0), 0.0)
        p_s = jnp.where(keep, pltpu.roll(pc, d, 0), 1.0)
        hc = pc * h_s + hc
        pc = pc * p_s
        d *= 2
    h_in = h_ref[...]
    seg_end = hc + pc * h_in
    carry = jnp.where(sub == 0, h_in, pltpu.roll(seg_end, 1, 0))
    h_ref[...] = jnp.broadcast_to(seg_end[SUBLANES - 1:SUBLANES, :], h_in.shape)
    return jnp.concatenate([h + p * carry for h, p in zip(hs, ps)], axis=0)


def _mixer_kernel(x_ref, win_ref, rgcw_ref, rgcb_ref, wri_ref, bri_ref, lam_ref,
                  wgate_ref, bgate_ref, ng_ref, hones_ref, cmask_ref, bdmask_ref,
                  cvw_ref, cvb_ref, gng_ref, gnb_ref, gones_ref,
                  wout_ref, lng_ref, lnb_ref,
                  o_ref,
                  mix_ref, hista_ref, histc_ref, h_ref, st_ref):
    tm = MIX_TM

    @pl.when(pl.program_id(1) == 0)
    def _():
        hista_ref[...] = jnp.zeros_like(hista_ref)
        histc_ref[...] = jnp.zeros_like(histc_ref)
        h_ref[...] = jnp.zeros_like(h_ref)
        st_ref[...] = jnp.zeros_like(st_ref)

    x = x_ref[...]
    xb = x.astype(BF16)

    xy = _dot(xb, win_ref[:, COL_RG:COL_RG + 2 * W_A])
    ya = xy[:, W_A:]
    xc = _causal_conv(xy[:, :W_A], hista_ref, rgcw_ref, rgcb_ref, RG_CONV_K)
    ri = _dot(xc.astype(BF16), wri_ref[...]) + bri_ref[...]
    r = _sigmoid(ri[:, :W_A])
    ig = _sigmoid(ri[:, W_A:])
    log_a = (-RG_C * r) * _softplus(-lam_ref[...])
    a = jnp.exp(log_a)
    u = jnp.sqrt(_neg_expm1_of_2log(a, log_a)) * (ig * xc)
    h = _linear_recurrence(a, u, h_ref)
    gelu = 0.5 * ya * (1.0 + jnp.tanh(0.7978845608028654 * (ya + 0.044715 * (ya * ya * ya))))
    mix_ref[:, 0:W_A] = (h * gelu).astype(BF16)

    gl = _dot(xb, win_ref[:, COL_GLA:COL_GLA + N_GLA])
    q = gl[:, 0:QK_PAD]
    kk = gl[:, QK_PAD:2 * QK_PAD]
    v = gl[:, 2 * QK_PAD:2 * QK_PAD + W_B]
    g = gl[:, 2 * QK_PAD + W_B:2 * QK_PAD + 2 * W_B]
    lr = gl[:, 2 * QK_PAD + 2 * W_B:]
    gate_pre = _dot(lr.astype(BF16), wgate_ref[...]) + bgate_ref[...]
    la = (jnp.minimum(gate_pre, 0.0) - jnp.log(1.0 + jnp.exp(-jnp.abs(gate_pre)))) * (1.0 / GLA_TAU)
    cs = [_rows(la, 0)]
    for gi in range(1, MIX_G):
        cs.append(cs[-1] + _rows(la, gi))
    sub = lax.broadcasted_iota(jnp.int32, (SUBLANES, QK_PAD), 0)
    seg_in_chunk = sub & (SEG_PER_CHUNK - 1)
    seg_tot = cs[-1]
    fix = jnp.zeros_like(seg_tot)
    for j in range(1, SEG_PER_CHUNK):
        fix = fix + jnp.where(seg_in_chunk >= j, pltpu.roll(seg_tot, j, 0), 0.0)
    bc = jnp.concatenate([c + fix for c in cs], axis=0)
    blast = cs[-1] + fix
    for j in range(1, SEG_PER_CHUNK):
        blast = jnp.where(seg_in_chunk == SEG_PER_CHUNK - 1 - j,
                          pltpu.roll(blast, SUBLANES - j, 0), blast)
    blast_t = jnp.concatenate([blast] * MIX_G, axis=0)
    decay = jnp.exp(blast)

    qd = (q * (GLA_DK ** -0.5)) * jnp.exp(bc)
    ki = kk * jnp.exp(-bc)
    kt = kk * jnp.exp(blast_t - bc)

    lane_q = lax.broadcasted_iota(jnp.int32, (SUBLANES, QK_PAD), 1)
    lane_v = lax.broadcasted_iota(jnp.int32, (SUBLANES, W_B), 1)
    head_q = [(lane_q >= hh * GLA_DK) & (lane_q < (hh + 1) * GLA_DK) for hh in range(GLA_HEADS)]
    head_v = [(lane_v >= hh * GLA_DV) & (lane_v < (hh + 1) * GLA_DV) for hh in range(GLA_HEADS)]
    in_chunk = [(sub >= c * SEG_PER_CHUNK) & (sub < (c + 1) * SEG_PER_CHUNK) for c in range(MIX_NCH)]

    def keep(val, m8):
        return jnp.concatenate([jnp.where(m8, _rows(val, gi), 0.0) for gi in range(MIX_G)], axis=0)

    ki_heads = jnp.concatenate([keep(ki, m) for m in head_q], axis=0).astype(BF16)
    s = _dot_nt(qd.astype(BF16), ki_heads)
    allowed = cmask_ref[...] > 0.5
    p = jnp.concatenate([jnp.where(allowed, s[:, hh * tm:(hh + 1) * tm], 0.0)
                         for hh in range(GLA_HEADS)], axis=1).astype(BF16)
    v_heads = jnp.concatenate([keep(v, m) for m in head_v], axis=0).astype(BF16)
    o = _dot(p, v_heads)

    kt_chunks = jnp.concatenate([keep(kt, m) for m in in_chunk], axis=1).astype(BF16)
    inc = _dot_tn(v.astype(BF16), kt_chunks)
    bd = bdmask_ref[...]
    st = st_ref[...]
    states = []
    for c in range(MIX_NCH):
        states.append(st.astype(BF16))
        row = (c + 1) * SEG_PER_CHUNK - 1
        st = st * decay[row:row + 1, :] + inc[:, c * QK_PAD:(c + 1) * QK_PAD] * bd
    st_ref[...] = st
    qd_chunks = jnp.concatenate([keep(qd, m) for m in in_chunk], axis=1).astype(BF16)
    o = o + _dot_nt(qd_chunks, jnp.concatenate(states, axis=1))

    ms = _group_sum(o * o, hones_ref[...]) * (1.0 / GLA_DV)
    o = o * lax.rsqrt(ms + EPS) * ng_ref[...]
    mix_ref[:, W_A:W_A + W_B] = (o * (g * _sigmoid(g))).astype(BF16)

    cc = _dot(xb, win_ref[:, COL_CV:COL_CV + 2 * W_C])
    acc = _causal_conv(cc[:, :W_C] * _sigmoid(cc[:, W_C:]), histc_ref, cvw_ref, cvb_ref, CONV_K)
    gsz = W_C // CONV_GROUPS
    mu = _group_sum(acc, gones_ref[...]) * (1.0 / gsz)
    dd = acc - mu
    var = _group_sum(dd * dd, gones_ref[...]) * (1.0 / gsz)
    un = dd * lax.rsqrt(var + EPS) * gng_ref[...] + gnb_ref[...]
    mix_ref[:, W_A + W_B:] = (un * _sigmoid(un)).astype(BF16)

    y = _dot(mix_ref[...], wout_ref[...])
    o_ref[...] = _layer_norm(ALPHA * x + y, lng_ref[...], lnb_ref[...])


def _mixer(x3d, p, layer):
    bsz, s, _ = x3d.shape
    tm = MIX_TM
    xspec = pl.BlockSpec((None, tm, D_MODEL), lambda b, t: (b, t, 0))
    return pl.pallas_call(
        _mixer_kernel,
        grid=(bsz, s // tm),
        in_specs=[
            xspec,
            _const_spec((D_MODEL, N_IN_PAD), layer),
            _const_spec((RG_CONV_K, W_A), layer),
            _const_spec((1, W_A), layer),
            _const_spec((W_A, 2 * W_A), layer),
            _const_spec((1, 2 * W_A), layer),
            _const_spec((1, W_A), layer),
            _const_spec((LR_PAD, QK_PAD), layer),
            _const_spec((1, QK_PAD), layer),
            _const_spec((1, W_B), layer),
            _shared_spec((W_B, W_B)),
            _shared_spec((tm, tm)),
            _shared_spec((W_B, QK_PAD)),
            _const_spec((CONV_K, W_C), layer),
            _const_spec((1, W_C), layer),
            _const_spec((1, W_C), layer),
            _const_spec((1, W_C), layer),
            _shared_spec((W_C, W_C)),
            _const_spec((D_MODEL, D_MODEL), layer),
            _const_spec((1, D_MODEL), layer),
            _const_spec((1, D_MODEL), layer),
        ],
        out_specs=xspec,
        out_shape=jax.ShapeDtypeStruct(x3d.shape, F32),
        scratch_shapes=[
            pltpu.VMEM((tm, D_MODEL), BF16),
            pltpu.VMEM(((RG_CONV_K - 1) * SUBLANES, W_A), F32),
            pltpu.VMEM(((CONV_K - 1) * SUBLANES, W_C), F32),
            pltpu.VMEM((SUBLANES, W_A), F32),
            pltpu.VMEM((W_B, QK_PAD), F32),
        ],
        compiler_params=pltpu.CompilerParams(
            dimension_semantics=("arbitrary", "arbitrary"), vmem_limit_bytes=VMEM_LIMIT),
        name="mixer_ln",
    )(x3d, p["mix_w_in"], p["rg_conv_w"], p["rg_conv_b"], p["rg_w_ri"], p["rg_b_ri"],
      p["rg_lambda"], p["gla_w_gate"], p["gla_b_gate"], p["gla_norm_g"], p["head_ones"],
      p["chunk_causal"], p["state_blockdiag"],
      p["cv_dw_w"], p["cv_dw_b"], p["cv_gn_g"], p["cv_gn_b"], p["group_ones"],
      p["mix_w_out"], p["ln2_g"], p["ln2_b"])


def _kv_kernel(m_ref, w_ref, k_ref, v_ref):
    kv = _dot(m_ref[...].astype(BF16), w_ref[...])
    k_ref[...] = kv[:, :D_MODEL].astype(BF16)
    v_ref[...] = kv[:, D_MODEL:].astype(BF16)


def _kv_proj(mem3d, w_kv, layer):
    bsz, m, _ = mem3d.shape
    mspec = pl.BlockSpec((None, m, D_MODEL), lambda b: (b, 0, 0))
    return pl.pallas_call(
        _kv_kernel,
        grid=(bsz,),
        in_specs=[mspec, _const_spec((D_MODEL, 2 * D_MODEL), layer)],
        out_specs=[mspec, mspec],
        out_shape=[jax.ShapeDtypeStruct(mem3d.shape, BF16)] * 2,
        compiler_params=pltpu.CompilerParams(
            dimension_semantics=("arbitrary",), vmem_limit_bytes=VMEM_LIMIT),
        name="xattn_kv",
    )(mem3d, w_kv)


def _xattn_kernel(x_ref, k_ref, v_ref, wq_ref, wo_ref, g_ref, b_ref, o_ref, att_ref):
    x = x_ref[...]
    q = _dot(x.astype(BF16), wq_ref[...]).astype(BF16)
    for hh in range(XA_HEADS):
        c0 = hh * XA_HEAD_DIM
        s = _dot_nt(q[:, c0:c0 + XA_HEAD_DIM], k_ref[:, c0:c0 + XA_HEAD_DIM]) * (XA_HEAD_DIM ** -0.5)
        e = jnp.exp(s - jnp.max(s, axis=-1, keepdims=True))
        p = e / jnp.sum(e, axis=-1, keepdims=True)
        att_ref[:, c0:c0 + XA_HEAD_DIM] = _dot(p.astype(BF16), v_ref[:, c0:c0 + XA_HEAD_DIM]).astype(BF16)
    y = _dot(att_ref[...], wo_ref[...])
    o_ref[...] = _layer_norm(ALPHA * x + y, g_ref[...], b_ref[...])


def _xattn(x3d, k3d, v3d, w_q, w_o, g, b, layer):
    bsz, s, _ = x3d.shape
    m = k3d.shape[1]
    tm = XA_TM
    xspec = pl.BlockSpec((None, tm, D_MODEL), lambda bi, t: (bi, t, 0))
    mspec = pl.BlockSpec((None, m, D_MODEL), lambda bi, t: (bi, 0, 0))
    return pl.pallas_call(
        _xattn_kernel,
        grid=(bsz, s // tm),
        in_specs=[
            xspec, mspec, mspec,
            _const_spec((D_MODEL, D_MODEL), layer),
            _const_spec((D_MODEL, D_MODEL), layer),
            _const_spec((1, D_MODEL), layer),
            _const_spec((1, D_MODEL), layer),
        ],
        out_specs=xspec,
        out_shape=jax.ShapeDtypeStruct(x3d.shape, F32),
        scratch_shapes=[pltpu.VMEM((tm, D_MODEL), BF16)],
        compiler_params=pltpu.CompilerParams(
            dimension_semantics=("arbitrary", "arbitrary"), vmem_limit_bytes=VMEM_LIMIT),
        name="xattn_ln",
    )(x3d, k3d, v3d, w_q, w_o, g, b)


def _block_ones(n, group):
    idx = np.arange(n) // group
    return jnp.asarray((idx[:, None] == idx[None, :]).astype(np.float32), dtype=BF16)


def _chunk_causal_mask():
    row = np.arange(MIX_TM)
    step = (row % SUBLANES) * MIX_G + row // SUBLANES
    chunk = step // GLA_CHUNK
    ok = (chunk[:, None] == chunk[None, :]) & (step[:, None] >= step[None, :])
    return jnp.asarray(ok.astype(np.float32))


def _state_blockdiag_mask():
    vh = np.arange(W_B) // GLA_DV
    kh = np.arange(QK_PAD) // GLA_DK
    return jnp.asarray((vh[:, None] == kh[None, :]).astype(np.float32))


def _prepare_params(mix_w_in, rg_conv_b, rg_w_r, rg_b_r, rg_w_i, rg_b_i, rg_lambda,
                    gla_w_gate, gla_b_gate, gla_norm_g, cv_dw_b, cv_gn_g, cv_gn_b):
    L = DEPTH
    nqk = GLA_HEADS * GLA_DK
    c_q = 2 * W_A
    c_k = c_q + nqk
    c_v = c_k + nqk
    c_g = c_v + W_B
    c_lr = c_g + W_B
    c_cv = c_lr + GLA_RANK

    def zeros(n):
        return jnp.zeros((L, D_MODEL, n), mix_w_in.dtype)

    w_in = jnp.concatenate([
        mix_w_in[:, :, :c_q],
        mix_w_in[:, :, c_q:c_k], zeros(QK_PAD - nqk),
        mix_w_in[:, :, c_k:c_v], zeros(QK_PAD - nqk),
        mix_w_in[:, :, c_v:c_lr],
        mix_w_in[:, :, c_lr:c_cv], zeros(LR_PAD - GLA_RANK),
        mix_w_in[:, :, c_cv:],
    ], axis=-1).astype(BF16)

    eye = jnp.eye(RG_HEADS, dtype=rg_w_r.dtype)

    def blockdiag(w):
        return jnp.einsum("lhij,hg->lhigj", w, eye).reshape(L, W_A, W_A)

    w_ri = jnp.concatenate([blockdiag(rg_w_r), blockdiag(rg_w_i)], axis=-1).astype(BF16)
    b_ri = jnp.concatenate([rg_b_r, rg_b_i], axis=-1)[:, None, :]
    w_gate = jnp.pad(gla_w_gate, ((0, 0), (0, LR_PAD - GLA_RANK), (0, QK_PAD - nqk))).astype(BF16)
    b_gate = jnp.pad(gla_b_gate, ((0, 0), (0, QK_PAD - nqk)))[:, None, :]
    return {
        "mix_w_in": w_in,
        "rg_conv_b": rg_conv_b[:, None, :],
        "rg_w_ri": w_ri,
        "rg_b_ri": b_ri,
        "rg_lambda": rg_lambda[:, None, :],
        "gla_w_gate": w_gate,
        "gla_b_gate": b_gate,
        "gla_norm_g": jnp.tile(gla_norm_g, (1, GLA_HEADS))[:, None, :],
        "head_ones": _block_ones(W_B, GLA_DV),
        "chunk_causal": _chunk_causal_mask(),
        "state_blockdiag": _state_blockdiag_mask(),
        "cv_dw_b": cv_dw_b[:, None, :],
        "cv_gn_g": cv_gn_g[:, None, :],
        "cv_gn_b": cv_gn_b[:, None, :],
        "group_ones": _block_ones(W_C, W_C // CONV_GROUPS),
    }


def kernel(x, mem, ffn1_w_in, ffn1_w_out, ln1_g, ln1_b, mix_w_in, rg_conv_w, rg_conv_b, rg_w_r, rg_b_r, rg_w_i, rg_b_i, rg_lambda, gla_w_gate, gla_b_gate, gla_norm_g, cv_dw_w, cv_dw_b, cv_gn_g, cv_gn_b, mix_w_out, ln2_g, ln2_b, xa_w_q, xa_w_kv, xa_w_o, ln3_g, ln3_b, ffn2_w_in, ffn2_w_out, ln4_g, ln4_b):
    bsz, s, dm = x.shape
    p = _prepare_params(mix_w_in, rg_conv_b, rg_w_r, rg_b_r, rg_w_i, rg_b_i, rg_lambda,
                        gla_w_gate, gla_b_gate, gla_norm_g, cv_dw_b, cv_gn_g, cv_gn_b)
    p.update({
        "rg_conv_w": rg_conv_w, "cv_dw_w": cv_dw_w,
        "mix_w_out": mix_w_out.astype(BF16),
        "ln2_g": ln2_g[:, None, :], "ln2_b": ln2_b[:, None, :],
    })
    f1_in, f1_out = ffn1_w_in.astype(BF16), ffn1_w_out.astype(BF16)
    f2_in, f2_out = ffn2_w_in.astype(BF16), ffn2_w_out.astype(BF16)
    w_q, w_kv, w_o = xa_w_q.astype(BF16), xa_w_kv.astype(BF16), xa_w_o.astype(BF16)
    ln1g, ln1b = ln1_g[:, None, :], ln1_b[:, None, :]
    ln3g, ln3b = ln3_g[:, None, :], ln3_b[:, None, :]
    ln4g, ln4b = ln4_g[:, None, :], ln4_b[:, None, :]

    nt = s // MIX_TM
    x = x.reshape(bsz, nt, SUBLANES, MIX_G, dm).swapaxes(2, 3).reshape(bsz, s, dm)
    for l in range(DEPTH):
        x = _ffn(x.reshape(bsz * s, dm), f1_in, f1_out, ln1g, ln1b, l).reshape(bsz, s, dm)
        x = _mixer(x, p, l)
        k3d, v3d = _kv_proj(mem, w_kv, l)
        x = _xattn(x, k3d, v3d, w_q, w_o, ln3g, ln3b, l)
        x = _ffn(x.reshape(bsz * s, dm), f2_in, f2_out, ln4g, ln4b, l).reshape(bsz, s, dm)
    return x.reshape(bsz, nt, MIX_G, SUBLANES, dm).swapaxes(2, 3).reshape(bsz, s, dm)
```

```python
import numpy as np
import jax
import jax.numpy as jnp
from jax import lax
from jax.experimental import pallas as pl
from jax.experimental.pallas import tpu as pltpu

D_MODEL = 1024
BATCH = 8
SEQ = 2048
DEPTH = 4
MEM_LEN = 256
D_FF = 2816
W_A = 384
W_B = 384
W_C = 256
RG_BLOCK = 64
RG_HEADS = W_A // RG_BLOCK
RG_CONV_K = 4
RG_C = 8.0
GLA_HEADS = 4
GLA_DV = W_B // GLA_HEADS
GLA_DK = GLA_DV // 2
GLA_RANK = 16
GLA_TAU = 16.0
GLA_CHUNK = 64
CONV_K = 31
CONV_GROUPS = 4
XA_HEADS = 4
XA_HEAD_DIM = D_MODEL // XA_HEADS
ALPHA = (2.0 * DEPTH) ** 0.25
EPS = 1e-5

BF16 = jnp.bfloat16
F32 = jnp.float32

SUBLANES = 8

QK_PAD = 256
LR_PAD = 128
COL_RG = 0
COL_GLA = 2 * W_A
N_GLA = 2 * QK_PAD + 2 * W_B + LR_PAD
COL_CV = COL_GLA + N_GLA
N_IN_PAD = COL_CV + 2 * W_C

FFN_TM = 512
FFN_FC = 256
MIX_TM = 256
MIX_G = MIX_TM // SUBLANES
MIX_NCH = MIX_TM // GLA_CHUNK
SEG_PER_CHUNK = GLA_CHUNK // MIX_G
XA_TM = 512
LN_ROWS = 32
CV_TAPS_PER_STAGE = 4
MIX_STAGE_ORDER = ("cv", "ln", "rg", "ln", "cv", "ln", "gla", "ln", "ln", "cv", "ln", "cv", "ln",
                   "rg", "ln", "cv", "gla", "cv", "gla", "cv", "gla", "rg", "cv", "gla", "gla",
                   "rg", "cv", "gla", "rg", "gla", "cv", "gla", "rg", "cv", "gla")
VMEM_LIMIT = 56 * 1024 * 1024

assert GLA_CHUNK % MIX_G == 0 and CONV_K - 1 <= MIX_G


def _dot(a, b):
    return jnp.dot(a, b, preferred_element_type=F32)


def _dot_nt(a, b):
    return lax.dot_general(a, b, (((1,), (1,)), ((), ())), preferred_element_type=F32)


def _dot_tn(a, b):
    return lax.dot_general(a, b, (((0,), (0,)), ((), ())), preferred_element_type=F32)


def _sigmoid(x):
    return 1.0 / (1.0 + jnp.exp(-x))


def _softplus(x):
    return jnp.maximum(x, 0.0) + jnp.log1p(jnp.exp(-jnp.abs(x)))


def _neg_expm1_of_2log(a, log_a):
    x2 = 2.0 * log_a
    sq = a * a
    sqm1 = sq - 1.0
    em1 = jnp.where(sq == 1.0, x2, sqm1 * x2 / jnp.log(sq))
    return -jnp.where(sqm1 == -1.0, -1.0, em1)


def _layer_norm(z, g, b):
    mu = jnp.mean(z, axis=-1, keepdims=True)
    d = z - mu
    var = jnp.mean(d * d, axis=-1, keepdims=True)
    return d * lax.rsqrt(var + EPS) * g + b


def _group_sum(y, ones_bf16):
    hi = y.astype(BF16)
    lo = (y - hi.astype(F32)).astype(BF16)
    return _dot(hi, ones_bf16) + _dot(lo, ones_bf16)


def _rows(v, g, n=1):
    return v[g * SUBLANES:(g + n) * SUBLANES, :]


def _lagged_layer_norm(z_ref, g_ref, b_ref, o_ref, tm):
    def stage(blk):
        rows = slice(blk * LN_ROWS, (blk + 1) * LN_ROWS)
        o_ref[rows, :] = _layer_norm(z_ref[rows, :], g_ref[...], b_ref[...])
    return [lambda blk=blk: stage(blk) for blk in range(tm // LN_ROWS)]


def _ffn_kernel(x_ref, win_ref, wout_ref, g_ref, b_ref, o_ref, act_ref, z_ref):
    i = pl.program_id(0)
    n_tiles = pl.num_programs(0) - 1

    @pl.when(i == 0)
    def _():
        z_ref[...] = jnp.zeros_like(z_ref)

    @pl.when(i < n_tiles)
    def _():
        x = x_ref[...]
        xb = x.astype(BF16)
        ln_stages = _lagged_layer_norm(z_ref, g_ref, b_ref, o_ref, FFN_TM)
        n_chunks = D_FF // FFN_FC
        done = 0
        for c in range(n_chunks):
            w_gate = win_ref[:, c * FFN_FC:(c + 1) * FFN_FC].astype(BF16)
            w_up = win_ref[:, D_FF + c * FFN_FC:D_FF + (c + 1) * FFN_FC].astype(BF16)
            gate = _dot(xb, w_gate)
            up = _dot(xb, w_up)
            act_ref[:, c * FFN_FC:(c + 1) * FFN_FC] = (gate * _sigmoid(gate) * up).astype(BF16)
            upto = ((c + 1) * len(ln_stages)) // n_chunks
            for st in ln_stages[done:upto]:
                st()
            done = upto
        y = _dot(act_ref[...], wout_ref[...].astype(BF16))
        z_ref[...] = ALPHA * x + 0.5 * y

    @pl.when(i == n_tiles)
    def _():
        o_ref[...] = _layer_norm(z_ref[...], g_ref[...], b_ref[...])


def _const_spec(shape, layer):
    nd = len(shape)
    return pl.BlockSpec((None,) + tuple(shape), lambda *_: (layer,) + (0,) * nd,
                        pipeline_mode=pl.Buffered(1))


def _shared_spec(shape):
    nd = len(shape)
    return pl.BlockSpec(tuple(shape), lambda *_: (0,) * nd, pipeline_mode=pl.Buffered(1))


def _ffn(x2d, w_in, w_out, g, b, layer):
    t = x2d.shape[0]
    n_tiles = t // FFN_TM
    return pl.pallas_call(
        _ffn_kernel,
        grid=(n_tiles + 1,),
        in_specs=[
            pl.BlockSpec((FFN_TM, D_MODEL), lambda i: (jnp.minimum(i, n_tiles - 1), 0)),
            _const_spec((D_MODEL, 2 * D_FF), layer),
            _const_spec((D_FF, D_MODEL), layer),
            _const_spec((1, D_MODEL), layer),
            _const_spec((1, D_MODEL), layer),
        ],
        out_specs=pl.BlockSpec((FFN_TM, D_MODEL), lambda i: (jnp.maximum(i - 1, 0), 0)),
        out_shape=jax.ShapeDtypeStruct((t, D_MODEL), F32),
        scratch_shapes=[pltpu.VMEM((FFN_TM, D_FF), BF16), pltpu.VMEM((FFN_TM, D_MODEL), F32)],
        compiler_params=pltpu.CompilerParams(
            dimension_semantics=("arbitrary",), vmem_limit_bytes=VMEM_LIMIT),
        name="ffn_ln",
    )(x2d, w_in, w_out, g, b)


def _shifted_history(cur, hist_ref, n_hist):
    tail = _rows(cur, MIX_G - n_hist, n_hist)
    sub = lax.broadcasted_iota(jnp.int32, tail.shape, 0) & (SUBLANES - 1)
    merged = jnp.where(sub == SUBLANES - 1, hist_ref[...], tail)
    hist_ref[...] = tail
    return [pltpu.roll(_rows(merged, i), 1, 0) for i in range(n_hist)]


def _causal_conv(cur, hist_ref, w_ref, b_ref, k_taps):
    ext = jnp.concatenate(_shifted_history(cur, hist_ref, k_taps - 1) + [cur], axis=0)
    acc = b_ref[...]
    for k in range(k_taps):
        acc = acc + w_ref[k:k + 1, :] * _rows(ext, k, MIX_G)
    return acc


def _linear_recurrence(a, u, h_ref):
    hs, ps = [_rows(u, 0)], [_rows(a, 0)]
    for g in range(1, MIX_G):
        ag = _rows(a, g)
        hs.append(ag * hs[-1] + _rows(u, g))
        ps.append(ag * ps[-1])
    hc, pc = hs[-1], ps[-1]
    sub = lax.broadcasted_iota(jnp.int32, hc.shape, 0)
    d = 1
    while d < SUBLANES:
        keep = sub >= d
        h_s = jnp.where(keep, pltpu.roll(hc, d, 0), 0.0)
        p_s = jnp.where(keep, pltpu.roll(pc, d, 0), 1.0)
        hc = pc * h_s + hc
        pc = pc * p_s
        d *= 2
    h_in = h_ref[...]
    seg_end = hc + pc * h_in
    carry = jnp.where(sub == 0, h_in, pltpu.roll(seg_end, 1, 0))
    h_ref[...] = jnp.broadcast_to(seg_end[SUBLANES - 1:SUBLANES, :], h_in.shape)
    return jnp.concatenate([h + p * carry for h, p in zip(hs, ps)], axis=0)


def _mixer_kernel(x_ref, win_ref, rgcw_ref, rgcb_ref, wri_ref, bri_ref, lam_ref,
                  wgate_ref, bgate_ref, ng_ref, hones_ref, cmask_ref, bdmask_ref,
                  cvw_ref, cvb_ref, gng_ref, gnb_ref, gones_ref,
                  wout_ref, lng_ref, lnb_ref,
                  o_ref,
                  mix_ref, hista_ref, histc_ref, h_ref, st_ref, z_ref):
    tm = MIX_TM
    i = pl.program_id(0)
    tile = jnp.minimum(i, pl.num_programs(0) - 2)

    @pl.when(i == 0)
    def _():
        z_ref[...] = jnp.zeros_like(z_ref)

    @pl.when(lax.rem(tile, SEQ // MIX_TM) == 0)
    def _():
        hista_ref[...] = jnp.zeros_like(hista_ref)
        histc_ref[...] = jnp.zeros_like(histc_ref)
        h_ref[...] = jnp.zeros_like(h_ref)
        st_ref[...] = jnp.zeros_like(st_ref)

    x = x_ref[...]
    xb = x.astype(BF16)

    def rg_group():
        xy = _dot(xb, win_ref[:, COL_RG:COL_RG + 2 * W_A])
        yield
        ya = xy[:, W_A:]
        xc = _causal_conv(xy[:, :W_A], hista_ref, rgcw_ref, rgcb_ref, RG_CONV_K)
        ri = _dot(xc.astype(BF16), wri_ref[...]) + bri_ref[...]
        yield
        r = _sigmoid(ri[:, :W_A])
        ig = _sigmoid(ri[:, W_A:])
        yield
        log_a = (-RG_C * r) * _softplus(-lam_ref[...])
        a = jnp.exp(log_a)
        u = jnp.sqrt(_neg_expm1_of_2log(a, log_a)) * (ig * xc)
        yield
        h = _linear_recurrence(a, u, h_ref)
        yield
        gelu = 0.5 * ya * (1.0 + jnp.tanh(0.7978845608028654 * (ya + 0.044715 * (ya * ya * ya))))
        mix_ref[:, 0:W_A] = (h * gelu).astype(BF16)

    def gla_group():
        gl = _dot(xb, win_ref[:, COL_GLA:COL_GLA + N_GLA])
        yield
        q = gl[:, 0:QK_PAD]
        kk = gl[:, QK_PAD:2 * QK_PAD]
        v = gl[:, 2 * QK_PAD:2 * QK_PAD + W_B]
        g = gl[:, 2 * QK_PAD + W_B:2 * QK_PAD + 2 * W_B]
        lr = gl[:, 2 * QK_PAD + 2 * W_B:]
        gate_pre = _dot(lr.astype(BF16), wgate_ref[...]) + bgate_ref[...]
        la = (jnp.minimum(gate_pre, 0.0) - jnp.log(1.0 + jnp.exp(-jnp.abs(gate_pre)))) * (1.0 / GLA_TAU)
        cs = [_rows(la, 0)]
        for gi in range(1, MIX_G):
            cs.append(cs[-1] + _rows(la, gi))
        sub = lax.broadcasted_iota(jnp.int32, (SUBLANES, QK_PAD), 0)
        seg_in_chunk = sub & (SEG_PER_CHUNK - 1)
        seg_tot = cs[-1]
        fix = jnp.zeros_like(seg_tot)
        for j in range(1, SEG_PER_CHUNK):
            fix = fix + jnp.where(seg_in_chunk >= j, pltpu.roll(seg_tot, j, 0), 0.0)
        bc = jnp.concatenate([c + fix for c in cs], axis=0)
        blast = cs[-1] + fix
        for j in range(1, SEG_PER_CHUNK):
            blast = jnp.where(seg_in_chunk == SEG_PER_CHUNK - 1 - j,
                              pltpu.roll(blast, SUBLANES - j, 0), blast)
        blast_t = jnp.concatenate([blast] * MIX_G, axis=0)
        decay = jnp.exp(blast)
        yield
        qd = (q * (GLA_DK ** -0.5)) * jnp.exp(bc)
        ki = kk * jnp.exp(-bc)
        kt = kk * jnp.exp(blast_t - bc)

        lane_q = lax.broadcasted_iota(jnp.int32, (SUBLANES, QK_PAD), 1)
        lane_v = lax.broadcasted_iota(jnp.int32, (SUBLANES, W_B), 1)
        head_q = [(lane_q >= hh * GLA_DK) & (lane_q < (hh + 1) * GLA_DK) for hh in range(GLA_HEADS)]
        head_v = [(lane_v >= hh * GLA_DV) & (lane_v < (hh + 1) * GLA_DV) for hh in range(GLA_HEADS)]
        in_chunk = [(sub >= c * SEG_PER_CHUNK) & (sub < (c + 1) * SEG_PER_CHUNK) for c in range(MIX_NCH)]

        def keep(val, m8):
            return jnp.concatenate([jnp.where(m8, _rows(val, gi), 0.0) for gi in range(MIX_G)], axis=0)

        ki_heads = jnp.concatenate([keep(ki, m) for m in head_q], axis=0).astype(BF16)
        yield
        s = _dot_nt(qd.astype(BF16), ki_heads)
        v_heads = jnp.concatenate([keep(v, m) for m in head_v], axis=0).astype(BF16)
        yield
        allowed = cmask_ref[...] > 0.5
        p = jnp.concatenate([jnp.where(allowed, s[:, hh * tm:(hh + 1) * tm], 0.0)
                             for hh in range(GLA_HEADS)], axis=1).astype(BF16)
        kt_chunks = jnp.concatenate([keep(kt, m) for m in in_chunk], axis=1).astype(BF16)
        yield
        o = _dot(p, v_heads)
        inc = _dot_tn(v.astype(BF16), kt_chunks)
        yield
        bd = bdmask_ref[...]
        st = st_ref[...]
        states = []
        for c in range(MIX_NCH):
            states.append(st.astype(BF16))
            row = (c + 1) * SEG_PER_CHUNK - 1
            st = st * decay[row:row + 1, :] + inc[:, c * QK_PAD:(c + 1) * QK_PAD] * bd
        st_ref[...] = st
        qd_chunks = jnp.concatenate([keep(qd, m) for m in in_chunk], axis=1).astype(BF16)
        yield
        o = o + _dot_nt(qd_chunks, jnp.concatenate(states, axis=1))
        yield
        ms = _group_sum(o * o, hones_ref[...]) * (1.0 / GLA_DV)
        yield
        o = o * lax.rsqrt(ms + EPS) * ng_ref[...]
        mix_ref[:, W_A:W_A + W_B] = (o * (g * _sigmoid(g))).astype(BF16)

    def conv_group():
        cc = _dot(xb, win_ref[:, COL_CV:COL_CV + 2 * W_C])
        yield
        cur = cc[:, :W_C] * _sigmoid(cc[:, W_C:])
        ext = jnp.concatenate(_shifted_history(cur, histc_ref, CONV_K - 1) + [cur], axis=0)
        acc = cvb_ref[...]
        for k in range(CONV_K):
            acc = acc + cvw_ref[k:k + 1, :] * _rows(ext, k, MIX_G)
            if k % CV_TAPS_PER_STAGE == CV_TAPS_PER_STAGE - 1:
                yield
        gsz = W_C // CONV_GROUPS
        mu = _group_sum(acc, gones_ref[...]) * (1.0 / gsz)
        yield
        dd = acc - mu
        var = _group_sum(dd * dd, gones_ref[...]) * (1.0 / gsz)
        yield
        un = dd * lax.rsqrt(var + EPS) * gng_ref[...] + gnb_ref[...]
        mix_ref[:, W_A + W_B:] = (un * _sigmoid(un)).astype(BF16)

    def lagged_ln():
        for st in _lagged_layer_norm(z_ref, lng_ref, lnb_ref, o_ref, tm):
            st()
            yield

    groups = {"rg": rg_group(), "gla": gla_group(), "cv": conv_group(), "ln": lagged_ln()}
    for name in MIX_STAGE_ORDER:
        next(groups[name], None)
    for gen in groups.values():
        for _ in gen:
            pass

    y = _dot(mix_ref[...], wout_ref[...])
    z_ref[...] = ALPHA * x + y


def _mixer(x2d, p, layer):
    t = x2d.shape[0]
    tm = MIX_TM
    n_tiles = t // tm
    return pl.pallas_call(
        _mixer_kernel,
        grid=(n_tiles + 1,),
        in_specs=[
            pl.BlockSpec((tm, D_MODEL), lambda i: (jnp.minimum(i, n_tiles - 1), 0)),
            _const_spec((D_MODEL, N_IN_PAD), layer),
            _const_spec((RG_CONV_K, W_A), layer),
            _const_spec((1, W_A), layer),
            _const_spec((W_A, 2 * W_A), layer),
            _const_spec((1, 2 * W_A), layer),
            _const_spec((1, W_A), layer),
            _const_spec((LR_PAD, QK_PAD), layer),
            _const_spec((1, QK_PAD), layer),
            _const_spec((1, W_B), layer),
            _shared_spec((W_B, W_B)),
            _shared_spec((tm, tm)),
            _shared_spec((W_B, QK_PAD)),
            _const_spec((CONV_K, W_C), layer),
            _const_spec((1, W_C), layer),
            _const_spec((1, W_C), layer),
            _const_spec((1, W_C), layer),
            _shared_spec((W_C, W_C)),
            _const_spec((D_MODEL, D_MODEL), layer),
            _const_spec((1, D_MODEL), layer),
            _const_spec((1, D_MODEL), layer),
        ],
        out_specs=pl.BlockSpec((tm, D_MODEL), lambda i: (jnp.maximum(i - 1, 0), 0)),
        out_shape=jax.ShapeDtypeStruct((t, D_MODEL), F32),
        scratch_shapes=[
            pltpu.VMEM((tm, D_MODEL), BF16),
            pltpu.VMEM(((RG_CONV_K - 1) * SUBLANES, W_A), F32),
            pltpu.VMEM(((CONV_K - 1) * SUBLANES, W_C), F32),
            pltpu.VMEM((SUBLANES, W_A), F32),
            pltpu.VMEM((W_B, QK_PAD), F32),
            pltpu.VMEM((tm, D_MODEL), F32),
        ],
        compiler_params=pltpu.CompilerParams(
            dimension_semantics=("arbitrary",), vmem_limit_bytes=VMEM_LIMIT),
        name="mixer_ln",
    )(x2d, p["mix_w_in"], p["rg_conv_w"], p["rg_conv_b"], p["rg_w_ri"], p["rg_b_ri"],
      p["rg_lambda"], p["gla_w_gate"], p["gla_b_gate"], p["gla_norm_g"], p["head_ones"],
      p["chunk_causal"], p["state_blockdiag"],
      p["cv_dw_w"], p["cv_dw_b"], p["cv_gn_g"], p["cv_gn_b"], p["group_ones"],
      p["mix_w_out"], p["ln2_g"], p["ln2_b"])


def _kv_kernel(m_ref, w_ref, k_ref, v_ref):
    kv = _dot(m_ref[...].astype(BF16), w_ref[...].astype(BF16))
    k_ref[...] = kv[:, :D_MODEL].astype(BF16)
    v_ref[...] = kv[:, D_MODEL:].astype(BF16)


def _kv_proj(mem3d, w_kv, layer):
    bsz, m, _ = mem3d.shape
    mspec = pl.BlockSpec((None, m, D_MODEL), lambda b: (b, 0, 0))
    return pl.pallas_call(
        _kv_kernel,
        grid=(bsz,),
        in_specs=[mspec, _const_spec((D_MODEL, 2 * D_MODEL), layer)],
        out_specs=[mspec, mspec],
        out_shape=[jax.ShapeDtypeStruct(mem3d.shape, BF16)] * 2,
        compiler_params=pltpu.CompilerParams(
            dimension_semantics=("arbitrary",), vmem_limit_bytes=VMEM_LIMIT),
        name="xattn_kv",
    )(mem3d, w_kv)


def _xattn_kernel(x_ref, k_ref, v_ref, wq_ref, wo_ref, g_ref, b_ref, o_ref, att_ref, z_ref):
    i = pl.program_id(0)
    n_tiles = pl.num_programs(0) - 1

    @pl.when(i == 0)
    def _():
        z_ref[...] = jnp.zeros_like(z_ref)

    @pl.when(i < n_tiles)
    def _():
        x = x_ref[...]
        ln_stages = _lagged_layer_norm(z_ref, g_ref, b_ref, o_ref, XA_TM)
        per_phase = len(ln_stages) // (XA_HEADS + 2)
        q = _dot(x.astype(BF16), wq_ref[...].astype(BF16)).astype(BF16)
        for st in ln_stages[:per_phase]:
            st()
        for hh in range(XA_HEADS):
            c0 = hh * XA_HEAD_DIM
            s = _dot_nt(q[:, c0:c0 + XA_HEAD_DIM], k_ref[:, c0:c0 + XA_HEAD_DIM]) * (XA_HEAD_DIM ** -0.5)
            e = jnp.exp(s - jnp.max(s, axis=-1, keepdims=True))
            p = e / jnp.sum(e, axis=-1, keepdims=True)
            att_ref[:, c0:c0 + XA_HEAD_DIM] = _dot(p.astype(BF16), v_ref[:, c0:c0 + XA_HEAD_DIM]).astype(BF16)
            for st in ln_stages[(hh + 1) * per_phase:(hh + 2) * per_phase]:
                st()
        for st in ln_stages[(XA_HEADS + 1) * per_phase:]:
            st()
        y = _dot(att_ref[...], wo_ref[...].astype(BF16))
        z_ref[...] = ALPHA * x + y

    @pl.when(i == n_tiles)
    def _():
        o_ref[...] = _layer_norm(z_ref[...], g_ref[...], b_ref[...])


def _xattn(x2d, k3d, v3d, w_q, w_o, g, b, layer):
    t = x2d.shape[0]
    bsz, m, _ = k3d.shape
    tm = XA_TM
    n_tiles = t // tm
    tiles_per_batch = n_tiles // bsz
    mspec = pl.BlockSpec((None, m, D_MODEL),
                         lambda i: (jnp.minimum(i, n_tiles - 1) // tiles_per_batch, 0, 0))
    return pl.pallas_call(
        _xattn_kernel,
        grid=(n_tiles + 1,),
        in_specs=[
            pl.BlockSpec((tm, D_MODEL), lambda i: (jnp.minimum(i, n_tiles - 1), 0)),
            mspec, mspec,
            _const_spec((D_MODEL, D_MODEL), layer),
            _const_spec((D_MODEL, D_MODEL), layer),
            _const_spec((1, D_MODEL), layer),
            _const_spec((1, D_MODEL), layer),
        ],
        out_specs=pl.BlockSpec((tm, D_MODEL), lambda i: (jnp.maximum(i - 1, 0), 0)),
        out_shape=jax.ShapeDtypeStruct((t, D_MODEL), F32),
        scratch_shapes=[pltpu.VMEM((tm, D_MODEL), BF16), pltpu.VMEM((tm, D_MODEL), F32)],
        compiler_params=pltpu.CompilerParams(
            dimension_semantics=("arbitrary",), vmem_limit_bytes=VMEM_LIMIT),
        name="xattn_ln",
    )(x2d, k3d, v3d, w_q, w_o, g, b)


def _block_ones(n, group):
    idx = np.arange(n) // group
    return jnp.asarray((idx[:, None] == idx[None, :]).astype(np.float32), dtype=BF16)


def _chunk_causal_mask():
    row = np.arange(MIX_TM)
    step = (row % SUBLANES) * MIX_G + row // SUBLANES
    chunk = step // GLA_CHUNK
    ok = (chunk[:, None] == chunk[None, :]) & (step[:, None] >= step[None, :])
    return jnp.asarray(ok.astype(np.float32))


def _state_blockdiag_mask():
    vh = np.arange(W_B) // GLA_DV
    kh = np.arange(QK_PAD) // GLA_DK
    return jnp.asarray((vh[:, None] == kh[None, :]).astype(np.float32))


def _prepare_params(mix_w_in, rg_conv_b, rg_w_r, rg_b_r, rg_w_i, rg_b_i, rg_lambda,
                    gla_w_gate, gla_b_gate, gla_norm_g, cv_dw_b, cv_gn_g, cv_gn_b):
    L = DEPTH
    nqk = GLA_HEADS * GLA_DK
    c_q = 2 * W_A
    c_k = c_q + nqk
    c_v = c_k + nqk
    c_g = c_v + W_B
    c_lr = c_g + W_B
    c_cv = c_lr + GLA_RANK

    def zeros(n):
        return jnp.zeros((L, D_MODEL, n), mix_w_in.dtype)

    w_in = jnp.concatenate([
        mix_w_in[:, :, :c_q],
        mix_w_in[:, :, c_q:c_k], zeros(QK_PAD - nqk),
        mix_w_in[:, :, c_k:c_v], zeros(QK_PAD - nqk),
        mix_w_in[:, :, c_v:c_lr],
        mix_w_in[:, :, c_lr:c_cv], zeros(LR_PAD - GLA_RANK),
        mix_w_in[:, :, c_cv:],
    ], axis=-1).astype(BF16)

    eye = jnp.eye(RG_HEADS, dtype=rg_w_r.dtype)

    def blockdiag(w):
        return jnp.einsum("lhij,hg->lhigj", w, eye).reshape(L, W_A, W_A)

    w_ri = jnp.concatenate([blockdiag(rg_w_r), blockdiag(rg_w_i)], axis=-1).astype(BF16)
    b_ri = jnp.concatenate([rg_b_r, rg_b_i], axis=-1)[:, None, :]
    w_gate = jnp.pad(gla_w_gate, ((0, 0), (0, LR_PAD - GLA_RANK), (0, QK_PAD - nqk))).astype(BF16)
    b_gate = jnp.pad(gla_b_gate, ((0, 0), (0, QK_PAD - nqk)))[:, None, :]
    return {
        "mix_w_in": w_in,
        "rg_conv_b": rg_conv_b[:, None, :],
        "rg_w_ri": w_ri,
        "rg_b_ri": b_ri,
        "rg_lambda": rg_lambda[:, None, :],
        "gla_w_gate": w_gate,
        "gla_b_gate": b_gate,
        "gla_norm_g": jnp.tile(gla_norm_g, (1, GLA_HEADS))[:, None, :],
        "head_ones": _block_ones(W_B, GLA_DV),
        "chunk_causal": _chunk_causal_mask(),
        "state_blockdiag": _state_blockdiag_mask(),
        "cv_dw_b": cv_dw_b[:, None, :],
        "cv_gn_g": cv_gn_g[:, None, :],
        "cv_gn_b": cv_gn_b[:, None, :],
        "group_ones": _block_ones(W_C, W_C // CONV_GROUPS),
    }


def kernel(x, mem, ffn1_w_in, ffn1_w_out, ln1_g, ln1_b, mix_w_in, rg_conv_w, rg_conv_b, rg_w_r, rg_b_r, rg_w_i, rg_b_i, rg_lambda, gla_w_gate, gla_b_gate, gla_norm_g, cv_dw_w, cv_dw_b, cv_gn_g, cv_gn_b, mix_w_out, ln2_g, ln2_b, xa_w_q, xa_w_kv, xa_w_o, ln3_g, ln3_b, ffn2_w_in, ffn2_w_out, ln4_g, ln4_b):
    bsz, s, dm = x.shape
    p = _prepare_params(mix_w_in, rg_conv_b, rg_w_r, rg_b_r, rg_w_i, rg_b_i, rg_lambda,
                        gla_w_gate, gla_b_gate, gla_norm_g, cv_dw_b, cv_gn_g, cv_gn_b)
    p.update({
        "rg_conv_w": rg_conv_w, "cv_dw_w": cv_dw_w,
        "mix_w_out": mix_w_out.astype(BF16),
        "ln2_g": ln2_g[:, None, :], "ln2_b": ln2_b[:, None, :],
    })
    ln1g, ln1b = ln1_g[:, None, :], ln1_b[:, None, :]
    ln3g, ln3b = ln3_g[:, None, :], ln3_b[:, None, :]
    ln4g, ln4b = ln4_g[:, None, :], ln4_b[:, None, :]

    nt = s // MIX_TM
    x = x.reshape(bsz, nt, SUBLANES, MIX_G, dm).swapaxes(2, 3).reshape(bsz, s, dm)
    x = x.reshape(bsz * s, dm)
    for l in range(DEPTH):
        x = _ffn(x, ffn1_w_in, ffn1_w_out, ln1g, ln1b, l)
        x = _mixer(x, p, l)
        k3d, v3d = _kv_proj(mem, xa_w_kv, l)
        x = _xattn(x, k3d, v3d, xa_w_q, xa_w_o, ln3g, ln3b, l)
        x = _ffn(x, ffn2_w_in, ffn2_w_out, ln4g, ln4b, l)
    return x.reshape(bsz, nt, MIX_G, SUBLANES, dm).swapaxes(2, 3).reshape(bsz, s, dm)
```

```python
import numpy as np
import jax
import jax.numpy as jnp
from jax import lax
from jax.experimental import pallas as pl
from jax.experimental.pallas import tpu as pltpu

D_MODEL = 1024
BATCH = 8
SEQ = 2048
DEPTH = 4
MEM_LEN = 256
D_FF = 2816
W_A = 384
W_B = 384
W_C = 256
RG_BLOCK = 64
RG_HEADS = W_A // RG_BLOCK
RG_CONV_K = 4
RG_C = 8.0
GLA_HEADS = 4
GLA_DV = W_B // GLA_HEADS
GLA_DK = GLA_DV // 2
GLA_RANK = 16
GLA_TAU = 16.0
GLA_CHUNK = 64
CONV_K = 31
CONV_GROUPS = 4
XA_HEADS = 4
XA_HEAD_DIM = D_MODEL // XA_HEADS
ALPHA = (2.0 * DEPTH) ** 0.25
EPS = 1e-5

BF16 = jnp.bfloat16
F32 = jnp.float32

SUBLANES = 8
LANES = 128

QK_PAD = 256
LR_PAD = 128
COL_RG = 0
COL_GLA = 2 * W_A
N_GLA = 2 * QK_PAD + 2 * W_B + LR_PAD
COL_CV = COL_GLA + N_GLA
N_IN_PAD = COL_CV + 2 * W_C

FFN_TM = 512
FFN_FC = 256
MIX_TM = 256
MIX_G = MIX_TM // SUBLANES
MIX_NCH = MIX_TM // GLA_CHUNK
SEG_PER_CHUNK = GLA_CHUNK // MIX_G
XA_TM = 512
LN_ROWS = 32
CV_TAPS_PER_STAGE = 4
MIX_STAGE_ORDER = ("cv", "ln", "rg", "ln", "cv", "ln", "gla", "ln", "ln", "cv", "ln", "cv", "ln",
                   "rg", "ln", "cv", "gla", "cv", "gla", "cv", "gla", "rg", "cv", "gla", "gla",
                   "rg", "cv", "gla", "rg", "gla", "cv", "gla", "rg", "cv", "gla")
VMEM_LIMIT = 56 * 1024 * 1024

assert GLA_CHUNK % MIX_G == 0 and CONV_K - 1 <= MIX_G


def _dot(a, b):
    return jnp.dot(a, b, preferred_element_type=F32)


def _dot_nt(a, b):
    return lax.dot_general(a, b, (((1,), (1,)), ((), ())), preferred_element_type=F32)


def _dot_tn(a, b):
    return lax.dot_general(a, b, (((0,), (0,)), ((), ())), preferred_element_type=F32)


def _sigmoid(x):
    return 1.0 / (1.0 + jnp.exp(-x))


def _softplus(x):
    return jnp.maximum(x, 0.0) + jnp.log1p(jnp.exp(-jnp.abs(x)))


def _neg_expm1_of_2log(a, log_a):
    x2 = 2.0 * log_a
    sq = a * a
    sqm1 = sq - 1.0
    em1 = jnp.where(sq == 1.0, x2, sqm1 * x2 / jnp.log(sq))
    return -jnp.where(sqm1 == -1.0, -1.0, em1)


def _layer_norm(z, g, b):
    mu = jnp.mean(z, axis=-1, keepdims=True)
    d = z - mu
    var = jnp.mean(d * d, axis=-1, keepdims=True)
    return d * lax.rsqrt(var + EPS) * g + b


def _group_sum(y, ones_bf16):
    hi = y.astype(BF16)
    lo = (y - hi.astype(F32)).astype(BF16)
    return _dot(hi, ones_bf16) + _dot(lo, ones_bf16)


def _rows(v, g, n=1):
    return v[g * SUBLANES:(g + n) * SUBLANES, :]


def _zero_token(v):
    bits = lax.bitcast_convert_type(v, jnp.int32)
    acc = bits[0:SUBLANES, :]
    for r in range(1, v.shape[0] // SUBLANES):
        acc = acc | bits[r * SUBLANES:(r + 1) * SUBLANES, :]
    tok = acc[:, 0:LANES]
    for j in range(1, v.shape[1] // LANES):
        tok = tok | acc[:, j * LANES:(j + 1) * LANES]
    return lax.shift_right_logical(lax.shift_right_logical(tok, 16), 16)


def _tie(v, tokens):
    if not tokens:
        return v
    tok = tokens[0]
    for t in tokens[1:]:
        tok = tok | t
    head = v[0:SUBLANES, 0:LANES] + lax.bitcast_convert_type(tok, F32)
    top = jnp.concatenate([head, v[0:SUBLANES, LANES:]], axis=1)
    return jnp.concatenate([top, v[SUBLANES:, :]], axis=0)


def _lagged_layer_norm(z_ref, g_ref, b_ref, o_ref, tm):
    def stage(blk):
        rows = slice(blk * LN_ROWS, (blk + 1) * LN_ROWS)
        out = _layer_norm(z_ref[rows, :], g_ref[...], b_ref[...])
        o_ref[rows, :] = out
        return _zero_token(out)
    return [lambda blk=blk: stage(blk) for blk in range(tm // LN_ROWS)]


def _ffn_kernel(x_ref, win_ref, wout_ref, g_ref, b_ref, o_ref, act_ref, z_ref):
    i = pl.program_id(0)
    n_tiles = pl.num_programs(0) - 1

    @pl.when(i == 0)
    def _():
        z_ref[...] = jnp.zeros_like(z_ref)

    @pl.when(i < n_tiles)
    def _():
        x = x_ref[...]
        xb = x.astype(BF16)
        ln_stages = _lagged_layer_norm(z_ref, g_ref, b_ref, o_ref, FFN_TM)
        n_chunks = D_FF // FFN_FC
        done = 0
        tokens = []
        for c in range(n_chunks):
            w_gate = win_ref[:, c * FFN_FC:(c + 1) * FFN_FC].astype(BF16)
            w_up = win_ref[:, D_FF + c * FFN_FC:D_FF + (c + 1) * FFN_FC].astype(BF16)
            gate = _tie(_dot(xb, w_gate), tokens)
            up = _dot(xb, w_up)
            act_ref[:, c * FFN_FC:(c + 1) * FFN_FC] = (gate * _sigmoid(gate) * up).astype(BF16)
            upto = min(len(ln_stages), ((c + 1) * len(ln_stages)) // (n_chunks - 1))
            tokens = [st() for st in ln_stages[done:upto]]
            done = upto
        y = _dot(act_ref[...], wout_ref[...].astype(BF16))
        z_ref[...] = ALPHA * x + 0.5 * y

    @pl.when(i == n_tiles)
    def _():
        o_ref[...] = _layer_norm(z_ref[...], g_ref[...], b_ref[...])


def _const_spec(shape, layer):
    nd = len(shape)
    return pl.BlockSpec((None,) + tuple(shape), lambda *_: (layer,) + (0,) * nd,
                        pipeline_mode=pl.Buffered(1))


def _shared_spec(shape):
    nd = len(shape)
    return pl.BlockSpec(tuple(shape), lambda *_: (0,) * nd, pipeline_mode=pl.Buffered(1))


def _ffn(x2d, w_in, w_out, g, b, layer):
    t = x2d.shape[0]
    n_tiles = t // FFN_TM
    return pl.pallas_call(
        _ffn_kernel,
        grid=(n_tiles + 1,),
        in_specs=[
            pl.BlockSpec((FFN_TM, D_MODEL), lambda i: (jnp.minimum(i, n_tiles - 1), 0)),
            _const_spec((D_MODEL, 2 * D_FF), layer),
            _const_spec((D_FF, D_MODEL), layer),
            _const_spec((1, D_MODEL), layer),
            _const_spec((1, D_MODEL), layer),
        ],
        out_specs=pl.BlockSpec((FFN_TM, D_MODEL), lambda i: (jnp.maximum(i - 1, 0), 0)),
        out_shape=jax.ShapeDtypeStruct((t, D_MODEL), F32),
        scratch_shapes=[pltpu.VMEM((FFN_TM, D_FF), BF16), pltpu.VMEM((FFN_TM, D_MODEL), F32)],
        compiler_params=pltpu.CompilerParams(
            dimension_semantics=("arbitrary",), vmem_limit_bytes=VMEM_LIMIT),
        name="ffn_ln",
    )(x2d, w_in, w_out, g, b)


def _shifted_history(cur, hist_ref, n_hist):
    tail = _rows(cur, MIX_G - n_hist, n_hist)
    sub = lax.broadcasted_iota(jnp.int32, tail.shape, 0) & (SUBLANES - 1)
    merged = jnp.where(sub == SUBLANES - 1, hist_ref[...], tail)
    hist_ref[...] = tail
    return [pltpu.roll(_rows(merged, i), 1, 0) for i in range(n_hist)]


def _causal_conv(cur, hist_ref, w_ref, b_ref, k_taps):
    ext = jnp.concatenate(_shifted_history(cur, hist_ref, k_taps - 1) + [cur], axis=0)
    acc = b_ref[...]
    for k in range(k_taps):
        acc = acc + w_ref[k:k + 1, :] * _rows(ext, k, MIX_G)
    return acc


def _linear_recurrence(a, u, h_ref):
    hs, ps = [_rows(u, 0)], [_rows(a, 0)]
    for g in range(1, MIX_G):
        ag = _rows(a, g)
        hs.append(ag * hs[-1] + _rows(u, g))
        ps.append(ag * ps[-1])
    hc, pc = hs[-1], ps[-1]
    sub = lax.broadcasted_iota(jnp.int32, hc.shape, 0)
    d = 1
    while d < SUBLANES:
        keep = sub >= d
        h_s = jnp.where(keep, pltpu.roll(hc, d, 0), 0.0)
        p_s = jnp.where(keep, pltpu.roll(pc, d, 0), 1.0)
        hc = pc * h_s + hc
        pc = pc * p_s
        d *= 2
    h_in = h_ref[...]
    seg_end = hc + pc * h_in
    carry = jnp.where(sub == 0, h_in, pltpu.roll(seg_end, 1, 0))
    h_ref[...] = jnp.broadcast_to(seg_end[SUBLANES - 1:SUBLANES, :], h_in.shape)
    return jnp.concatenate([h + p * carry for h, p in zip(hs, ps)], axis=0)


def _mixer_kernel(x_ref, win_ref, rgcw_ref, rgcb_ref, wri_ref, bri_ref, lam_ref,
                  wgate_ref, bgate_ref, ng_ref, hones_ref, cmask_ref, bdmask_ref,
                  cvw_ref, cvb_ref, gng_ref, gnb_ref, gones_ref,
                  wout_ref, lng_ref, lnb_ref,
                  o_ref,
                  mix_ref, hista_ref, histc_ref, h_ref, st_ref, z_ref):
    tm = MIX_TM
    i = pl.program_id(0)
    tile = jnp.minimum(i, pl.num_programs(0) - 2)

    @pl.when(i == 0)
    def _():
        z_ref[...] = jnp.zeros_like(z_ref)

    @pl.when(lax.rem(tile, SEQ // MIX_TM) == 0)
    def _():
        hista_ref[...] = jnp.zeros_like(hista_ref)
        histc_ref[...] = jnp.zeros_like(histc_ref)
        h_ref[...] = jnp.zeros_like(h_ref)
        st_ref[...] = jnp.zeros_like(st_ref)

    x = x_ref[...]
    xb = x.astype(BF16)

    def rg_group():
        xy = _dot(xb, win_ref[:, COL_RG:COL_RG + 2 * W_A])
        yield
        ya = xy[:, W_A:]
        xc = _causal_conv(xy[:, :W_A], hista_ref, rgcw_ref, rgcb_ref, RG_CONV_K)
        ri = _dot(xc.astype(BF16), wri_ref[...]) + bri_ref[...]
        yield
        r = _sigmoid(ri[:, :W_A])
        ig = _sigmoid(ri[:, W_A:])
        yield
        log_a = (-RG_C * r) * _softplus(-lam_ref[...])
        a = jnp.exp(log_a)
        u = jnp.sqrt(_neg_expm1_of_2log(a, log_a)) * (ig * xc)
        yield
        h = _linear_recurrence(a, u, h_ref)
        yield
        gelu = 0.5 * ya * (1.0 + jnp.tanh(0.7978845608028654 * (ya + 0.044715 * (ya * ya * ya))))
        mix_ref[:, 0:W_A] = (h * gelu).astype(BF16)

    def gla_group():
        gl = _dot(xb, win_ref[:, COL_GLA:COL_GLA + N_GLA])
        yield
        q = gl[:, 0:QK_PAD]
        kk = gl[:, QK_PAD:2 * QK_PAD]
        v = gl[:, 2 * QK_PAD:2 * QK_PAD + W_B]
        g = gl[:, 2 * QK_PAD + W_B:2 * QK_PAD + 2 * W_B]
        lr = gl[:, 2 * QK_PAD + 2 * W_B:]
        gate_pre = _dot(lr.astype(BF16), wgate_ref[...]) + bgate_ref[...]
        la = (jnp.minimum(gate_pre, 0.0) - jnp.log(1.0 + jnp.exp(-jnp.abs(gate_pre)))) * (1.0 / GLA_TAU)
        cs = [_rows(la, 0)]
        for gi in range(1, MIX_G):
            cs.append(cs[-1] + _rows(la, gi))
        sub = lax.broadcasted_iota(jnp.int32, (SUBLANES, QK_PAD), 0)
        seg_in_chunk = sub & (SEG_PER_CHUNK - 1)
        seg_tot = cs[-1]
        fix = jnp.zeros_like(seg_tot)
        for j in range(1, SEG_PER_CHUNK):
            fix = fix + jnp.where(seg_in_chunk >= j, pltpu.roll(seg_tot, j, 0), 0.0)
        bc = jnp.concatenate([c + fix for c in cs], axis=0)
        blast = cs[-1] + fix
        for j in range(1, SEG_PER_CHUNK):
            blast = jnp.where(seg_in_chunk == SEG_PER_CHUNK - 1 - j,
                              pltpu.roll(blast, SUBLANES - j, 0), blast)
        blast_t = jnp.concatenate([blast] * MIX_G, axis=0)
        decay = jnp.exp(blast)
        yield
        qd = (q * (GLA_DK ** -0.5)) * jnp.exp(bc)
        ki = kk * jnp.exp(-bc)
        kt = kk * jnp.exp(blast_t - bc)

        lane_q = lax.broadcasted_iota(jnp.int32, (SUBLANES, QK_PAD), 1)
        lane_v = lax.broadcasted_iota(jnp.int32, (SUBLANES, W_B), 1)
        head_q = [(lane_q >= hh * GLA_DK) & (lane_q < (hh + 1) * GLA_DK) for hh in range(GLA_HEADS)]
        head_v = [(lane_v >= hh * GLA_DV) & (lane_v < (hh + 1) * GLA_DV) for hh in range(GLA_HEADS)]
        in_chunk = [(sub >= c * SEG_PER_CHUNK) & (sub < (c + 1) * SEG_PER_CHUNK) for c in range(MIX_NCH)]

        def keep(val, m8):
            return jnp.concatenate([jnp.where(m8, _rows(val, gi), 0.0) for gi in range(MIX_G)], axis=0)

        ki_heads = jnp.concatenate([keep(ki, m) for m in head_q], axis=0).astype(BF16)
        yield
        s = _dot_nt(qd.astype(BF16), ki_heads)
        v_heads = jnp.concatenate([keep(v, m) for m in head_v], axis=0).astype(BF16)
        yield
        allowed = cmask_ref[...] > 0.5
        p = jnp.concatenate([jnp.where(allowed, s[:, hh * tm:(hh + 1) * tm], 0.0)
                             for hh in range(GLA_HEADS)], axis=1).astype(BF16)
        kt_chunks = jnp.concatenate([keep(kt, m) for m in in_chunk], axis=1).astype(BF16)
        yield
        o = _dot(p, v_heads)
        inc = _dot_tn(v.astype(BF16), kt_chunks)
        yield
        bd = bdmask_ref[...]
        st = st_ref[...]
        states = []
        for c in range(MIX_NCH):
            states.append(st.astype(BF16))
            row = (c + 1) * SEG_PER_CHUNK - 1
            st = st * decay[row:row + 1, :] + inc[:, c * QK_PAD:(c + 1) * QK_PAD] * bd
        st_ref[...] = st
        qd_chunks = jnp.concatenate([keep(qd, m) for m in in_chunk], axis=1).astype(BF16)
        yield
        o = o + _dot_nt(qd_chunks, jnp.concatenate(states, axis=1))
        yield
        ms = _group_sum(o * o, hones_ref[...]) * (1.0 / GLA_DV)
        yield
        o = o * lax.rsqrt(ms + EPS) * ng_ref[...]
        mix_ref[:, W_A:W_A + W_B] = (o * (g * _sigmoid(g))).astype(BF16)

    def conv_group():
        cc = _dot(xb, win_ref[:, COL_CV:COL_CV + 2 * W_C])
        yield
        cur = cc[:, :W_C] * _sigmoid(cc[:, W_C:])
        ext = jnp.concatenate(_shifted_history(cur, histc_ref, CONV_K - 1) + [cur], axis=0)
        acc = cvb_ref[...]
        for k in range(CONV_K):
            acc = acc + cvw_ref[k:k + 1, :] * _rows(ext, k, MIX_G)
            if k % CV_TAPS_PER_STAGE == CV_TAPS_PER_STAGE - 1:
                yield
        gsz = W_C // CONV_GROUPS
        mu = _group_sum(acc, gones_ref[...]) * (1.0 / gsz)
        yield
        dd = acc - mu
        var = _group_sum(dd * dd, gones_ref[...]) * (1.0 / gsz)
        yield
        un = dd * lax.rsqrt(var + EPS) * gng_ref[...] + gnb_ref[...]
        mix_ref[:, W_A + W_B:] = (un * _sigmoid(un)).astype(BF16)

    def lagged_ln():
        for st in _lagged_layer_norm(z_ref, lng_ref, lnb_ref, o_ref, tm):
            st()
            yield

    groups = {"rg": rg_group(), "gla": gla_group(), "cv": conv_group(), "ln": lagged_ln()}
    for name in MIX_STAGE_ORDER:
        next(groups[name], None)
    for gen in groups.values():
        for _ in gen:
            pass

    y = _dot(mix_ref[...], wout_ref[...])
    z_ref[...] = ALPHA * x + y


def _mixer(x2d, p, layer):
    t = x2d.shape[0]
    tm = MIX_TM
    n_tiles = t // tm
    return pl.pallas_call(
        _mixer_kernel,
        grid=(n_tiles + 1,),
        in_specs=[
            pl.BlockSpec((tm, D_MODEL), lambda i: (jnp.minimum(i, n_tiles - 1), 0)),
            _const_spec((D_MODEL, N_IN_PAD), layer),
            _const_spec((RG_CONV_K, W_A), layer),
            _const_spec((1, W_A), layer),
            _const_spec((W_A, 2 * W_A), layer),
            _const_spec((1, 2 * W_A), layer),
            _const_spec((1, W_A), layer),
            _const_spec((LR_PAD, QK_PAD), layer),
            _const_spec((1, QK_PAD), layer),
            _const_spec((1, W_B), layer),
            _shared_spec((W_B, W_B)),
            _shared_spec((tm, tm)),
            _shared_spec((W_B, QK_PAD)),
            _const_spec((CONV_K, W_C), layer),
            _const_spec((1, W_C), layer),
            _const_spec((1, W_C), layer),
            _const_spec((1, W_C), layer),
            _shared_spec((W_C, W_C)),
            _const_spec((D_MODEL, D_MODEL), layer),
            _const_spec((1, D_MODEL), layer),
            _const_spec((1, D_MODEL), layer),
        ],
        out_specs=pl.BlockSpec((tm, D_MODEL), lambda i: (jnp.maximum(i - 1, 0), 0)),
        out_shape=jax.ShapeDtypeStruct((t, D_MODEL), F32),
        scratch_shapes=[
            pltpu.VMEM((tm, D_MODEL), BF16),
            pltpu.VMEM(((RG_CONV_K - 1) * SUBLANES, W_A), F32),
            pltpu.VMEM(((CONV_K - 1) * SUBLANES, W_C), F32),
            pltpu.VMEM((SUBLANES, W_A), F32),
            pltpu.VMEM((W_B, QK_PAD), F32),
            pltpu.VMEM((tm, D_MODEL), F32),
        ],
        compiler_params=pltpu.CompilerParams(
            dimension_semantics=("arbitrary",), vmem_limit_bytes=VMEM_LIMIT),
        name="mixer_ln",
    )(x2d, p["mix_w_in"], p["rg_conv_w"], p["rg_conv_b"], p["rg_w_ri"], p["rg_b_ri"],
      p["rg_lambda"], p["gla_w_gate"], p["gla_b_gate"], p["gla_norm_g"], p["head_ones"],
      p["chunk_causal"], p["state_blockdiag"],
      p["cv_dw_w"], p["cv_dw_b"], p["cv_gn_g"], p["cv_gn_b"], p["group_ones"],
      p["mix_w_out"], p["ln2_g"], p["ln2_b"])


def _kv_kernel(m_ref, w_ref, k_ref, v_ref):
    kv = _dot(m_ref[...].astype(BF16), w_ref[...].astype(BF16))
    k_ref[...] = kv[:, :D_MODEL].astype(BF16)
    v_ref[...] = kv[:, D_MODEL:].astype(BF16)


def _kv_proj(mem3d, w_kv, layer):
    bsz, m, _ = mem3d.shape
    mspec = pl.BlockSpec((None, m, D_MODEL), lambda b: (b, 0, 0))
    return pl.pallas_call(
        _kv_kernel,
        grid=(bsz,),
        in_specs=[mspec, _const_spec((D_MODEL, 2 * D_MODEL), layer)],
        out_specs=[mspec, mspec],
        out_shape=[jax.ShapeDtypeStruct(mem3d.shape, BF16)] * 2,
        compiler_params=pltpu.CompilerParams(
            dimension_semantics=("arbitrary",), vmem_limit_bytes=VMEM_LIMIT),
        name="xattn_kv",
    )(mem3d, w_kv)


def _xattn_kernel(x_ref, k_ref, v_ref, wq_ref, wo_ref, g_ref, b_ref, o_ref, att_ref, z_ref):
    i = pl.program_id(0)
    n_tiles = pl.num_programs(0) - 1

    @pl.when(i == 0)
    def _():
        z_ref[...] = jnp.zeros_like(z_ref)

    @pl.when(i < n_tiles)
    def _():
        x = x_ref[...]
        ln_stages = _lagged_layer_norm(z_ref, g_ref, b_ref, o_ref, XA_TM)
        per_phase = len(ln_stages) // XA_HEADS
        q = _dot(x.astype(BF16), wq_ref[...].astype(BF16)).astype(BF16)
        for hh in range(XA_HEADS):
            tokens = [st() for st in ln_stages[hh * per_phase:(hh + 1) * per_phase]]
            c0 = hh * XA_HEAD_DIM
            s = _dot_nt(q[:, c0:c0 + XA_HEAD_DIM], k_ref[:, c0:c0 + XA_HEAD_DIM]) * (XA_HEAD_DIM ** -0.5)
            s = _tie(s, tokens)
            e = jnp.exp(s - jnp.max(s, axis=-1, keepdims=True))
            p = e / jnp.sum(e, axis=-1, keepdims=True)
            att_ref[:, c0:c0 + XA_HEAD_DIM] = _dot(p.astype(BF16), v_ref[:, c0:c0 + XA_HEAD_DIM]).astype(BF16)
        y = _dot(att_ref[...], wo_ref[...].astype(BF16))
        z_ref[...] = ALPHA * x + y

    @pl.when(i == n_tiles)
    def _():
        o_ref[...] = _layer_norm(z_ref[...], g_ref[...], b_ref[...])


def _xattn(x2d, k3d, v3d, w_q, w_o, g, b, layer):
    t = x2d.shape[0]
    bsz, m, _ = k3d.shape
    tm = XA_TM
    n_tiles = t // tm
    tiles_per_batch = n_tiles // bsz
    mspec = pl.BlockSpec((None, m, D_MODEL),
                         lambda i: (jnp.minimum(i, n_tiles - 1) // tiles_per_batch, 0, 0))
    return pl.pallas_call(
        _xattn_kernel,
        grid=(n_tiles + 1,),
        in_specs=[
            pl.BlockSpec((tm, D_MODEL), lambda i: (jnp.minimum(i, n_tiles - 1), 0)),
            mspec, mspec,
            _const_spec((D_MODEL, D_MODEL), layer),
            _const_spec((D_MODEL, D_MODEL), layer),
            _const_spec((1, D_MODEL), layer),
            _const_spec((1, D_MODEL), layer),
        ],
        out_specs=pl.BlockSpec((tm, D_MODEL), lambda i: (jnp.maximum(i - 1, 0), 0)),
        out_shape=jax.ShapeDtypeStruct((t, D_MODEL), F32),
        scratch_shapes=[pltpu.VMEM((tm, D_MODEL), BF16), pltpu.VMEM((tm, D_MODEL), F32)],
        compiler_params=pltpu.CompilerParams(
            dimension_semantics=("arbitrary",), vmem_limit_bytes=VMEM_LIMIT),
        name="xattn_ln",
    )(x2d, k3d, v3d, w_q, w_o, g, b)


def _block_ones(n, group):
    idx = np.arange(n) // group
    return jnp.asarray((idx[:, None] == idx[None, :]).astype(np.float32), dtype=BF16)


def _chunk_causal_mask():
    row = np.arange(MIX_TM)
    step = (row % SUBLANES) * MIX_G + row // SUBLANES
    chunk = step // GLA_CHUNK
    ok = (chunk[:, None] == chunk[None, :]) & (step[:, None] >= step[None, :])
    return jnp.asarray(ok.astype(np.float32))


def _state_blockdiag_mask():
    vh = np.arange(W_B) // GLA_DV
    kh = np.arange(QK_PAD) // GLA_DK
    return jnp.asarray((vh[:, None] == kh[None, :]).astype(np.float32))


def _prepare_params(mix_w_in, rg_conv_b, rg_w_r, rg_b_r, rg_w_i, rg_b_i, rg_lambda,
                    gla_w_gate, gla_b_gate, gla_norm_g, cv_dw_b, cv_gn_g, cv_gn_b):
    L = DEPTH
    nqk = GLA_HEADS * GLA_DK
    c_q = 2 * W_A
    c_k = c_q + nqk
    c_v = c_k + nqk
    c_g = c_v + W_B
    c_lr = c_g + W_B
    c_cv = c_lr + GLA_RANK

    def zeros(n):
        return jnp.zeros((L, D_MODEL, n), BF16)

    w16 = mix_w_in.astype(BF16)
    w_in = jnp.concatenate([
        w16[:, :, :c_q],
        w16[:, :, c_q:c_k], zeros(QK_PAD - nqk),
        w16[:, :, c_k:c_v], zeros(QK_PAD - nqk),
        w16[:, :, c_v:c_lr],
        w16[:, :, c_lr:c_cv], zeros(LR_PAD - GLA_RANK),
        w16[:, :, c_cv:],
    ], axis=-1)

    eye = jnp.eye(RG_HEADS, dtype=rg_w_r.dtype)

    def blockdiag(w):
        return jnp.einsum("lhij,hg->lhigj", w, eye).reshape(L, W_A, W_A)

    w_ri = jnp.concatenate([blockdiag(rg_w_r), blockdiag(rg_w_i)], axis=-1).astype(BF16)
    b_ri = jnp.concatenate([rg_b_r, rg_b_i], axis=-1)[:, None, :]
    w_gate = jnp.pad(gla_w_gate, ((0, 0), (0, LR_PAD - GLA_RANK), (0, QK_PAD - nqk))).astype(BF16)
    b_gate = jnp.pad(gla_b_gate, ((0, 0), (0, QK_PAD - nqk)))[:, None, :]
    return {
        "mix_w_in": w_in,
        "rg_conv_b": rg_conv_b[:, None, :],
        "rg_w_ri": w_ri,
        "rg_b_ri": b_ri,
        "rg_lambda": rg_lambda[:, None, :],
        "gla_w_gate": w_gate,
        "gla_b_gate": b_gate,
        "gla_norm_g": jnp.tile(gla_norm_g, (1, GLA_HEADS))[:, None, :],
        "head_ones": _block_ones(W_B, GLA_DV),
        "chunk_causal": _chunk_causal_mask(),
        "state_blockdiag": _state_blockdiag_mask(),
        "cv_dw_b": cv_dw_b[:, None, :],
        "cv_gn_g": cv_gn_g[:, None, :],
        "cv_gn_b": cv_gn_b[:, None, :],
        "group_ones": _block_ones(W_C, W_C // CONV_GROUPS),
    }


def kernel(x, mem, ffn1_w_in, ffn1_w_out, ln1_g, ln1_b, mix_w_in, rg_conv_w, rg_conv_b, rg_w_r, rg_b_r, rg_w_i, rg_b_i, rg_lambda, gla_w_gate, gla_b_gate, gla_norm_g, cv_dw_w, cv_dw_b, cv_gn_g, cv_gn_b, mix_w_out, ln2_g, ln2_b, xa_w_q, xa_w_kv, xa_w_o, ln3_g, ln3_b, ffn2_w_in, ffn2_w_out, ln4_g, ln4_b):
    bsz, s, dm = x.shape
    p = _prepare_params(mix_w_in, rg_conv_b, rg_w_r, rg_b_r, rg_w_i, rg_b_i, rg_lambda,
                        gla_w_gate, gla_b_gate, gla_norm_g, cv_dw_b, cv_gn_g, cv_gn_b)
    p.update({
        "rg_conv_w": rg_conv_w, "cv_dw_w": cv_dw_w,
        "mix_w_out": mix_w_out.astype(BF16),
        "ln2_g": ln2_g[:, None, :], "ln2_b": ln2_b[:, None, :],
    })
    ln1g, ln1b = ln1_g[:, None, :], ln1_b[:, None, :]
    ln3g, ln3b = ln3_g[:, None, :], ln3_b[:, None, :]
    ln4g, ln4b = ln4_g[:, None, :], ln4_b[:, None, :]

    nt = s // MIX_TM
    x = x.reshape(bsz, nt, SUBLANES, MIX_G, dm).swapaxes(2, 3).reshape(bsz, s, dm)
    x = x.reshape(bsz * s, dm)
    for l in range(DEPTH):
        x = _ffn(x, ffn1_w_in, ffn1_w_out, ln1g, ln1b, l)
        x = _mixer(x, p, l)
        k3d, v3d = _kv_proj(mem, xa_w_kv, l)
        x = _xattn(x, k3d, v3d, xa_w_q, xa_w_o, ln3g, ln3b, l)
        x = _ffn(x, ffn2_w_in, ffn2_w_out, ln4g, ln4b, l)
    return x.reshape(bsz, nt, MIX_G, SUBLANES, dm).swapaxes(2, 3).reshape(bsz, s, dm)
```

```python
import numpy as np
import jax
import jax.numpy as jnp
from jax import lax
from jax.experimental import pallas as pl
from jax.experimental.pallas import tpu as pltpu

D_MODEL = 1024
BATCH = 8
SEQ = 2048
DEPTH = 4
MEM_LEN = 256
D_FF = 2816
W_A = 384
W_B = 384
W_C = 256
RG_BLOCK = 64
RG_HEADS = W_A // RG_BLOCK
RG_CONV_K = 4
RG_C = 8.0
GLA_HEADS = 4
GLA_DV = W_B // GLA_HEADS
GLA_DK = GLA_DV // 2
GLA_RANK = 16
GLA_TAU = 16.0
GLA_CHUNK = 64
CONV_K = 31
CONV_GROUPS = 4
XA_HEADS = 4
XA_HEAD_DIM = D_MODEL // XA_HEADS
ALPHA = (2.0 * DEPTH) ** 0.25
EPS = 1e-5

BF16 = jnp.bfloat16
F32 = jnp.float32

SUBLANES = 8
LANES = 128

QK_PAD = 256
LR_PAD = 128
COL_RG = 0
COL_GLA = 2 * W_A
N_GLA = 2 * QK_PAD + 2 * W_B + LR_PAD
COL_CV = COL_GLA + N_GLA
N_IN_PAD = COL_CV + 2 * W_C

FFN_TM = 512
FFN_FC = 256
MIX_TM = 256
MIX_G = MIX_TM // SUBLANES
MIX_NCH = MIX_TM // GLA_CHUNK
SEG_PER_CHUNK = GLA_CHUNK // MIX_G
XA_TM = 512
LN_ROWS = 32
CV_TAPS_PER_STAGE = 4
MIX_STAGE_ORDER = ("cv", "ln", "rg", "ln", "cv", "ln", "gla", "ln", "ln", "cv", "ln", "cv", "ln",
                   "rg", "ln", "cv", "gla", "cv", "gla", "cv", "gla", "rg", "cv", "gla", "gla",
                   "rg", "cv", "gla", "rg", "gla", "cv", "gla", "rg", "cv", "gla")
VMEM_LIMIT = 56 * 1024 * 1024

assert GLA_CHUNK % MIX_G == 0 and CONV_K - 1 <= MIX_G


def _dot(a, b):
    return jnp.dot(a, b, preferred_element_type=F32)


def _dot_nt(a, b):
    return lax.dot_general(a, b, (((1,), (1,)), ((), ())), preferred_element_type=F32)


def _dot_tn(a, b):
    return lax.dot_general(a, b, (((0,), (0,)), ((), ())), preferred_element_type=F32)


def _sigmoid(x):
    return 1.0 / (1.0 + jnp.exp(-x))


def _softplus(x):
    return jnp.maximum(x, 0.0) + jnp.log1p(jnp.exp(-jnp.abs(x)))


def _neg_expm1_of_2log(a, log_a):
    x2 = 2.0 * log_a
    sq = a * a
    sqm1 = sq - 1.0
    em1 = jnp.where(sq == 1.0, x2, sqm1 * x2 / jnp.log(sq))
    return -jnp.where(sqm1 == -1.0, -1.0, em1)


def _layer_norm(z, g, b):
    mu = jnp.mean(z, axis=-1, keepdims=True)
    d = z - mu
    var = jnp.mean(d * d, axis=-1, keepdims=True)
    return d * lax.rsqrt(var + EPS) * g + b


def _group_sum(y, ones_bf16):
    hi = y.astype(BF16)
    lo = (y - hi.astype(F32)).astype(BF16)
    return _dot(hi, ones_bf16) + _dot(lo, ones_bf16)


def _rows(v, g, n=1):
    return v[g * SUBLANES:(g + n) * SUBLANES, :]


def _zero_token(v):
    bits = lax.bitcast_convert_type(v, jnp.int32)
    acc = bits[0:SUBLANES, :]
    for r in range(1, v.shape[0] // SUBLANES):
        acc = acc | bits[r * SUBLANES:(r + 1) * SUBLANES, :]
    tok = acc[:, 0:LANES]
    for j in range(1, v.shape[1] // LANES):
        tok = tok | acc[:, j * LANES:(j + 1) * LANES]
    return lax.shift_right_logical(lax.shift_right_logical(tok, 16), 16)


def _tie(v, tokens):
    if not tokens:
        return v
    tok = tokens[0]
    for t in tokens[1:]:
        tok = tok | t
    head = v[0:SUBLANES, 0:LANES] + lax.bitcast_convert_type(tok, F32)
    top = jnp.concatenate([head, v[0:SUBLANES, LANES:]], axis=1)
    return jnp.concatenate([top, v[SUBLANES:, :]], axis=0)


def _lagged_layer_norm(z_ref, g_ref, b_ref, o_ref, tm):
    def stage(blk):
        rows = slice(blk * LN_ROWS, (blk + 1) * LN_ROWS)
        out = _layer_norm(z_ref[rows, :], g_ref[...], b_ref[...])
        o_ref[rows, :] = out
        return _zero_token(out)
    return [lambda blk=blk: stage(blk) for blk in range(tm // LN_ROWS)]


def _ffn_kernel(x_ref, win_ref, wout_ref, g_ref, b_ref, o_ref, act_ref, z_ref):
    i = pl.program_id(0)
    n_tiles = pl.num_programs(0) - 1

    @pl.when(i == 0)
    def _():
        z_ref[...] = jnp.zeros_like(z_ref)

    @pl.when(i < n_tiles)
    def _():
        x = x_ref[...]
        xb = x.astype(BF16)
        ln_stages = _lagged_layer_norm(z_ref, g_ref, b_ref, o_ref, FFN_TM)
        n_chunks = D_FF // FFN_FC
        done = 0
        tokens = []
        for c in range(n_chunks):
            w_gate = win_ref[:, c * FFN_FC:(c + 1) * FFN_FC].astype(BF16)
            w_up = win_ref[:, D_FF + c * FFN_FC:D_FF + (c + 1) * FFN_FC].astype(BF16)
            gate = _tie(_dot(xb, w_gate), tokens)
            up = _dot(xb, w_up)
            act_ref[:, c * FFN_FC:(c + 1) * FFN_FC] = (gate * _sigmoid(gate) * up).astype(BF16)
            upto = min(len(ln_stages), ((c + 1) * len(ln_stages)) // (n_chunks - 1))
            tokens = [st() for st in ln_stages[done:upto]]
            done = upto
        y = _dot(act_ref[...], wout_ref[...].astype(BF16))
        z_ref[...] = ALPHA * x + 0.5 * y

    @pl.when(i == n_tiles)
    def _():
        o_ref[...] = _layer_norm(z_ref[...], g_ref[...], b_ref[...])


def _const_spec(shape, layer):
    nd = len(shape)
    return pl.BlockSpec((None,) + tuple(shape), lambda *_: (layer,) + (0,) * nd,
                        pipeline_mode=pl.Buffered(1))


def _shared_spec(shape):
    nd = len(shape)
    return pl.BlockSpec(tuple(shape), lambda *_: (0,) * nd, pipeline_mode=pl.Buffered(1))


def _ffn(x2d, w_in, w_out, g, b, layer):
    t = x2d.shape[0]
    n_tiles = t // FFN_TM
    return pl.pallas_call(
        _ffn_kernel,
        grid=(n_tiles + 1,),
        in_specs=[
            pl.BlockSpec((FFN_TM, D_MODEL), lambda i: (jnp.minimum(i, n_tiles - 1), 0)),
            _const_spec((D_MODEL, 2 * D_FF), layer),
            _const_spec((D_FF, D_MODEL), layer),
            _const_spec((1, D_MODEL), layer),
            _const_spec((1, D_MODEL), layer),
        ],
        out_specs=pl.BlockSpec((FFN_TM, D_MODEL), lambda i: (jnp.maximum(i - 1, 0), 0)),
        out_shape=jax.ShapeDtypeStruct((t, D_MODEL), F32),
        scratch_shapes=[pltpu.VMEM((FFN_TM, D_FF), BF16), pltpu.VMEM((FFN_TM, D_MODEL), F32)],
        compiler_params=pltpu.CompilerParams(
            dimension_semantics=("arbitrary",), vmem_limit_bytes=VMEM_LIMIT),
        name="ffn_ln",
    )(x2d, w_in, w_out, g, b)


def _shifted_history(cur, hist_ref, n_hist):
    tail = _rows(cur, MIX_G - n_hist, n_hist)
    sub = lax.broadcasted_iota(jnp.int32, tail.shape, 0) & (SUBLANES - 1)
    merged = jnp.where(sub == SUBLANES - 1, hist_ref[...], tail)
    hist_ref[...] = tail
    return [pltpu.roll(_rows(merged, i), 1, 0) for i in range(n_hist)]


def _causal_conv(cur, hist_ref, w_ref, b_ref, k_taps):
    ext = jnp.concatenate(_shifted_history(cur, hist_ref, k_taps - 1) + [cur], axis=0)
    acc = b_ref[...]
    for k in range(k_taps):
        acc = acc + w_ref[k:k + 1, :] * _rows(ext, k, MIX_G)
    return acc


def _linear_recurrence(a, u, h_ref):
    hs, ps = [_rows(u, 0)], [_rows(a, 0)]
    for g in range(1, MIX_G):
        ag = _rows(a, g)
        hs.append(ag * hs[-1] + _rows(u, g))
        ps.append(ag * ps[-1])
    hc, pc = hs[-1], ps[-1]
    sub = lax.broadcasted_iota(jnp.int32, hc.shape, 0)
    d = 1
    while d < SUBLANES:
        keep = sub >= d
        h_s = jnp.where(keep, pltpu.roll(hc, d, 0), 0.0)
        p_s = jnp.where(keep, pltpu.roll(pc, d, 0), 1.0)
        hc = pc * h_s + hc
        pc = pc * p_s
        d *= 2
    h_in = h_ref[...]
    seg_end = hc + pc * h_in
    carry = jnp.where(sub == 0, h_in, pltpu.roll(seg_end, 1, 0))
    h_ref[...] = jnp.broadcast_to(seg_end[SUBLANES - 1:SUBLANES, :], h_in.shape)
    return jnp.concatenate([h + p * carry for h, p in zip(hs, ps)], axis=0)


def _mixer_kernel(x_ref, win_ref, rgcw_ref, rgcb_ref, wri_ref, bri_ref, lam_ref,
                  wgate_ref, bgate_ref, ng_ref, hones_ref, cmask_ref, bdmask_ref,
                  cvw_ref, cvb_ref, gng_ref, gnb_ref, gones_ref,
                  wout_ref, lng_ref, lnb_ref,
                  o_ref,
                  mix_ref, hista_ref, histc_ref, h_ref, st_ref, z_ref):
    tm = MIX_TM
    i = pl.program_id(0)
    tile = jnp.minimum(i, pl.num_programs(0) - 2)

    @pl.when(i == 0)
    def _():
        z_ref[...] = jnp.zeros_like(z_ref)

    @pl.when(lax.rem(tile, SEQ // MIX_TM) == 0)
    def _():
        hista_ref[...] = jnp.zeros_like(hista_ref)
        histc_ref[...] = jnp.zeros_like(histc_ref)
        h_ref[...] = jnp.zeros_like(h_ref)
        st_ref[...] = jnp.zeros_like(st_ref)

    x = x_ref[...]
    xb = x.astype(BF16)

    def rg_group():
        xy = _dot(xb, win_ref[:, COL_RG:COL_RG + 2 * W_A])
        yield
        ya = xy[:, W_A:]
        xc = _causal_conv(xy[:, :W_A], hista_ref, rgcw_ref, rgcb_ref, RG_CONV_K)
        ri = _dot(xc.astype(BF16), wri_ref[...]) + bri_ref[...]
        yield
        r = _sigmoid(ri[:, :W_A])
        ig = _sigmoid(ri[:, W_A:])
        yield
        log_a = (-RG_C * r) * _softplus(-lam_ref[...])
        a = jnp.exp(log_a)
        u = jnp.sqrt(_neg_expm1_of_2log(a, log_a)) * (ig * xc)
        yield
        h = _linear_recurrence(a, u, h_ref)
        yield
        gelu = 0.5 * ya * (1.0 + jnp.tanh(0.7978845608028654 * (ya + 0.044715 * (ya * ya * ya))))
        mix_ref[:, 0:W_A] = (h * gelu).astype(BF16)

    def gla_group():
        gl = _dot(xb, win_ref[:, COL_GLA:COL_GLA + N_GLA])
        yield
        q = gl[:, 0:QK_PAD]
        kk = gl[:, QK_PAD:2 * QK_PAD]
        v = gl[:, 2 * QK_PAD:2 * QK_PAD + W_B]
        g = gl[:, 2 * QK_PAD + W_B:2 * QK_PAD + 2 * W_B]
        lr = gl[:, 2 * QK_PAD + 2 * W_B:]
        gate_pre = _dot(lr.astype(BF16), wgate_ref[...]) + bgate_ref[...]
        la = (jnp.minimum(gate_pre, 0.0) - jnp.log(1.0 + jnp.exp(-jnp.abs(gate_pre)))) * (1.0 / GLA_TAU)
        cs = [_rows(la, 0)]
        for gi in range(1, MIX_G):
            cs.append(cs[-1] + _rows(la, gi))
        sub = lax.broadcasted_iota(jnp.int32, (SUBLANES, QK_PAD), 0)
        seg_in_chunk = sub & (SEG_PER_CHUNK - 1)
        seg_tot = cs[-1]
        fix = jnp.zeros_like(seg_tot)
        for j in range(1, SEG_PER_CHUNK):
            fix = fix + jnp.where(seg_in_chunk >= j, pltpu.roll(seg_tot, j, 0), 0.0)
        bc = jnp.concatenate([c + fix for c in cs], axis=0)
        blast = cs[-1] + fix
        for j in range(1, SEG_PER_CHUNK):
            blast = jnp.where(seg_in_chunk == SEG_PER_CHUNK - 1 - j,
                              pltpu.roll(blast, SUBLANES - j, 0), blast)
        blast_t = jnp.concatenate([blast] * MIX_G, axis=0)
        decay = jnp.exp(blast)
        yield
        qd = (q * (GLA_DK ** -0.5)) * jnp.exp(bc)
        ki = kk * jnp.exp(-bc)
        kt = kk * jnp.exp(blast_t - bc)

        lane_q = lax.broadcasted_iota(jnp.int32, (SUBLANES, QK_PAD), 1)
        lane_v = lax.broadcasted_iota(jnp.int32, (SUBLANES, W_B), 1)
        head_q = [(lane_q >= hh * GLA_DK) & (lane_q < (hh + 1) * GLA_DK) for hh in range(GLA_HEADS)]
        head_v = [(lane_v >= hh * GLA_DV) & (lane_v < (hh + 1) * GLA_DV) for hh in range(GLA_HEADS)]
        in_chunk = [(sub >= c * SEG_PER_CHUNK) & (sub < (c + 1) * SEG_PER_CHUNK) for c in range(MIX_NCH)]

        def keep(val, m8):
            return jnp.concatenate([jnp.where(m8, _rows(val, gi), 0.0) for gi in range(MIX_G)], axis=0)

        ki_heads = jnp.concatenate([keep(ki, m) for m in head_q], axis=0).astype(BF16)
        yield
        s = _dot_nt(qd.astype(BF16), ki_heads)
        v_heads = jnp.concatenate([keep(v, m) for m in head_v], axis=0).astype(BF16)
        yield
        allowed = cmask_ref[...] > 0.5
        p = jnp.concatenate([jnp.where(allowed, s[:, hh * tm:(hh + 1) * tm], 0.0)
                             for hh in range(GLA_HEADS)], axis=1).astype(BF16)
        kt_chunks = jnp.concatenate([keep(kt, m) for m in in_chunk], axis=1).astype(BF16)
        yield
        o = _dot(p, v_heads)
        inc = _dot_tn(v.astype(BF16), kt_chunks)
        yield
        bd = bdmask_ref[...]
        st = st_ref[...]
        states = []
        for c in range(MIX_NCH):
            states.append(st.astype(BF16))
            row = (c + 1) * SEG_PER_CHUNK - 1
            st = st * decay[row:row + 1, :] + inc[:, c * QK_PAD:(c + 1) * QK_PAD] * bd
        st_ref[...] = st
        qd_chunks = jnp.concatenate([keep(qd, m) for m in in_chunk], axis=1).astype(BF16)
        yield
        o = o + _dot_nt(qd_chunks, jnp.concatenate(states, axis=1))
        yield
        ms = _group_sum(o * o, hones_ref[...]) * (1.0 / GLA_DV)
        yield
        o = o * lax.rsqrt(ms + EPS) * ng_ref[...]
        mix_ref[:, W_A:W_A + W_B] = (o * (g * _sigmoid(g))).astype(BF16)

    def conv_group():
        cc = _dot(xb, win_ref[:, COL_CV:COL_CV + 2 * W_C])
        yield
        cur = cc[:, :W_C] * _sigmoid(cc[:, W_C:])
        ext = jnp.concatenate(_shifted_history(cur, histc_ref, CONV_K - 1) + [cur], axis=0)
        acc = cvb_ref[...]
        for k in range(CONV_K):
            acc = acc + cvw_ref[k:k + 1, :] * _rows(ext, k, MIX_G)
            if k % CV_TAPS_PER_STAGE == CV_TAPS_PER_STAGE - 1:
                yield
        gsz = W_C // CONV_GROUPS
        mu = _group_sum(acc, gones_ref[...]) * (1.0 / gsz)
        yield
        dd = acc - mu
        var = _group_sum(dd * dd, gones_ref[...]) * (1.0 / gsz)
        yield
        un = dd * lax.rsqrt(var + EPS) * gng_ref[...] + gnb_ref[...]
        mix_ref[:, W_A + W_B:] = (un * _sigmoid(un)).astype(BF16)

    def lagged_ln():
        for st in _lagged_layer_norm(z_ref, lng_ref, lnb_ref, o_ref, tm):
            st()
            yield

    groups = {"rg": rg_group(), "gla": gla_group(), "cv": conv_group(), "ln": lagged_ln()}
    for name in MIX_STAGE_ORDER:
        next(groups[name], None)
    for gen in groups.values():
        for _ in gen:
            pass

    y = _dot(mix_ref[...], wout_ref[...])
    z_ref[...] = ALPHA * x + y


def _mixer(x2d, p, layer):
    t = x2d.shape[0]
    tm = MIX_TM
    n_tiles = t // tm
    return pl.pallas_call(
        _mixer_kernel,
        grid=(n_tiles + 1,),
        in_specs=[
            pl.BlockSpec((tm, D_MODEL), lambda i: (jnp.minimum(i, n_tiles - 1), 0)),
            _const_spec((D_MODEL, N_IN_PAD), layer),
            _const_spec((RG_CONV_K, W_A), layer),
            _const_spec((1, W_A), layer),
            _const_spec((W_A, 2 * W_A), layer),
            _const_spec((1, 2 * W_A), layer),
            _const_spec((1, W_A), layer),
            _const_spec((LR_PAD, QK_PAD), layer),
            _const_spec((1, QK_PAD), layer),
            _const_spec((1, W_B), layer),
            _shared_spec((W_B, W_B)),
            _shared_spec((tm, tm)),
            _shared_spec((W_B, QK_PAD)),
            _const_spec((CONV_K, W_C), layer),
            _const_spec((1, W_C), layer),
            _const_spec((1, W_C), layer),
            _const_spec((1, W_C), layer),
            _shared_spec((W_C, W_C)),
            _const_spec((D_MODEL, D_MODEL), layer),
            _const_spec((1, D_MODEL), layer),
            _const_spec((1, D_MODEL), layer),
        ],
        out_specs=pl.BlockSpec((tm, D_MODEL), lambda i: (jnp.maximum(i - 1, 0), 0)),
        out_shape=jax.ShapeDtypeStruct((t, D_MODEL), F32),
        scratch_shapes=[
            pltpu.VMEM((tm, D_MODEL), BF16),
            pltpu.VMEM(((RG_CONV_K - 1) * SUBLANES, W_A), F32),
            pltpu.VMEM(((CONV_K - 1) * SUBLANES, W_C), F32),
            pltpu.VMEM((SUBLANES, W_A), F32),
            pltpu.VMEM((W_B, QK_PAD), F32),
            pltpu.VMEM((tm, D_MODEL), F32),
        ],
        compiler_params=pltpu.CompilerParams(
            dimension_semantics=("arbitrary",), vmem_limit_bytes=VMEM_LIMIT),
        name="mixer_ln",
    )(x2d, p["mix_w_in"], p["rg_conv_w"], p["rg_conv_b"], p["rg_w_ri"], p["rg_b_ri"],
      p["rg_lambda"], p["gla_w_gate"], p["gla_b_gate"], p["gla_norm_g"], p["head_ones"],
      p["chunk_causal"], p["state_blockdiag"],
      p["cv_dw_w"], p["cv_dw_b"], p["cv_gn_g"], p["cv_gn_b"], p["group_ones"],
      p["mix_w_out"], p["ln2_g"], p["ln2_b"])


def _kv_kernel(m_ref, w_ref, k_ref, v_ref):
    kv = _dot(m_ref[...].astype(BF16), w_ref[...].astype(BF16))
    k_ref[...] = kv[:, :D_MODEL].astype(BF16)
    v_ref[...] = kv[:, D_MODEL:].astype(BF16)


def _kv_proj(mem3d, w_kv):
    bsz, m, _ = mem3d.shape
    depth = w_kv.shape[0]
    ospec = pl.BlockSpec((None, None, m, D_MODEL), lambda l, b: (l, b, 0, 0))
    return pl.pallas_call(
        _kv_kernel,
        grid=(depth, bsz),
        in_specs=[pl.BlockSpec((None, m, D_MODEL), lambda l, b: (b, 0, 0)),
                  pl.BlockSpec((None, D_MODEL, 2 * D_MODEL), lambda l, b: (l, 0, 0))],
        out_specs=[ospec, ospec],
        out_shape=[jax.ShapeDtypeStruct((depth,) + mem3d.shape, BF16)] * 2,
        compiler_params=pltpu.CompilerParams(
            dimension_semantics=("arbitrary", "arbitrary"), vmem_limit_bytes=VMEM_LIMIT),
        name="xattn_kv",
    )(mem3d, w_kv)


def _xattn_kernel(x_ref, k_ref, v_ref, wq_ref, wo_ref, g_ref, b_ref, o_ref, att_ref, z_ref):
    i = pl.program_id(0)
    n_tiles = pl.num_programs(0) - 1

    @pl.when(i == 0)
    def _():
        z_ref[...] = jnp.zeros_like(z_ref)

    @pl.when(i < n_tiles)
    def _():
        x = x_ref[...]
        ln_stages = _lagged_layer_norm(z_ref, g_ref, b_ref, o_ref, XA_TM)
        per_phase = len(ln_stages) // XA_HEADS
        q = _dot(x.astype(BF16), wq_ref[...].astype(BF16)).astype(BF16)
        for hh in range(XA_HEADS):
            tokens = [st() for st in ln_stages[hh * per_phase:(hh + 1) * per_phase]]
            c0 = hh * XA_HEAD_DIM
            s = _dot_nt(q[:, c0:c0 + XA_HEAD_DIM], k_ref[:, c0:c0 + XA_HEAD_DIM]) * (XA_HEAD_DIM ** -0.5)
            s = _tie(s, tokens)
            e = jnp.exp(s - jnp.max(s, axis=-1, keepdims=True))
            p = e / jnp.sum(e, axis=-1, keepdims=True)
            att_ref[:, c0:c0 + XA_HEAD_DIM] = _dot(p.astype(BF16), v_ref[:, c0:c0 + XA_HEAD_DIM]).astype(BF16)
        y = _dot(att_ref[...], wo_ref[...].astype(BF16))
        z_ref[...] = ALPHA * x + y

    @pl.when(i == n_tiles)
    def _():
        o_ref[...] = _layer_norm(z_ref[...], g_ref[...], b_ref[...])


def _xattn(x2d, k4d, v4d, w_q, w_o, g, b, layer):
    t = x2d.shape[0]
    _, bsz, m, _ = k4d.shape
    tm = XA_TM
    n_tiles = t // tm
    tiles_per_batch = n_tiles // bsz
    mspec = pl.BlockSpec((None, None, m, D_MODEL),
                         lambda i: (layer, jnp.minimum(i, n_tiles - 1) // tiles_per_batch, 0, 0))
    return pl.pallas_call(
        _xattn_kernel,
        grid=(n_tiles + 1,),
        in_specs=[
            pl.BlockSpec((tm, D_MODEL), lambda i: (jnp.minimum(i, n_tiles - 1), 0)),
            mspec, mspec,
            _const_spec((D_MODEL, D_MODEL), layer),
            _const_spec((D_MODEL, D_MODEL), layer),
            _const_spec((1, D_MODEL), layer),
            _const_spec((1, D_MODEL), layer),
        ],
        out_specs=pl.BlockSpec((tm, D_MODEL), lambda i: (jnp.maximum(i - 1, 0), 0)),
        out_shape=jax.ShapeDtypeStruct((t, D_MODEL), F32),
        scratch_shapes=[pltpu.VMEM((tm, D_MODEL), BF16), pltpu.VMEM((tm, D_MODEL), F32)],
        compiler_params=pltpu.CompilerParams(
            dimension_semantics=("arbitrary",), vmem_limit_bytes=VMEM_LIMIT),
        name="xattn_ln",
    )(x2d, k4d, v4d, w_q, w_o, g, b)


def _block_ones(n, group):
    idx = np.arange(n) // group
    return jnp.asarray((idx[:, None] == idx[None, :]).astype(np.float32), dtype=BF16)


def _chunk_causal_mask():
    row = np.arange(MIX_TM)
    step = (row % SUBLANES) * MIX_G + row // SUBLANES
    chunk = step // GLA_CHUNK
    ok = (chunk[:, None] == chunk[None, :]) & (step[:, None] >= step[None, :])
    return jnp.asarray(ok.astype(np.float32))


def _state_blockdiag_mask():
    vh = np.arange(W_B) // GLA_DV
    kh = np.arange(QK_PAD) // GLA_DK
    return jnp.asarray((vh[:, None] == kh[None, :]).astype(np.float32))


NQK = GLA_HEADS * GLA_DK
SRC_Q = 2 * W_A
SRC_K = SRC_Q + NQK
SRC_V = SRC_K + NQK
SRC_LR = SRC_V + 2 * W_B
SRC_CV = SRC_LR + GLA_RANK
N_IN_SRC = SRC_CV + 2 * W_C
RELAYOUT_ROWS = 256


def _relayout_kernel(w_ref, o_ref):
    w = w_ref[...]

    def zeros(n):
        return jnp.zeros((w.shape[0], n), w.dtype)

    o_ref[...] = jnp.concatenate([
        w[:, :SRC_Q],
        w[:, SRC_Q:SRC_K], zeros(QK_PAD - NQK),
        w[:, SRC_K:SRC_V], zeros(QK_PAD - NQK),
        w[:, SRC_V:SRC_LR],
        w[:, SRC_LR:SRC_CV], zeros(LR_PAD - GLA_RANK),
        w[:, SRC_CV:],
    ], axis=-1).astype(BF16)


def _relayout_mix_w_in(mix_w_in):
    depth = mix_w_in.shape[0]
    return pl.pallas_call(
        _relayout_kernel,
        grid=(depth, D_MODEL // RELAYOUT_ROWS),
        in_specs=[pl.BlockSpec((None, RELAYOUT_ROWS, N_IN_SRC), lambda l, r: (l, r, 0))],
        out_specs=pl.BlockSpec((None, RELAYOUT_ROWS, N_IN_PAD), lambda l, r: (l, r, 0)),
        out_shape=jax.ShapeDtypeStruct((depth, D_MODEL, N_IN_PAD), BF16),
        compiler_params=pltpu.CompilerParams(
            dimension_semantics=("arbitrary", "arbitrary"), vmem_limit_bytes=VMEM_LIMIT),
        name="mix_w_in_relayout",
    )(mix_w_in)


def _prepare_params(mix_w_in, rg_conv_b, rg_w_r, rg_b_r, rg_w_i, rg_b_i, rg_lambda,
                    gla_w_gate, gla_b_gate, gla_norm_g, cv_dw_b, cv_gn_g, cv_gn_b):
    L = DEPTH
    nqk = NQK
    w_in = _relayout_mix_w_in(mix_w_in)

    eye = jnp.eye(RG_HEADS, dtype=rg_w_r.dtype)

    def blockdiag(w):
        return jnp.einsum("lhij,hg->lhigj", w, eye).reshape(L, W_A, W_A)

    w_ri = jnp.concatenate([blockdiag(rg_w_r), blockdiag(rg_w_i)], axis=-1).astype(BF16)
    b_ri = jnp.concatenate([rg_b_r, rg_b_i], axis=-1)[:, None, :]
    w_gate = jnp.pad(gla_w_gate, ((0, 0), (0, LR_PAD - GLA_RANK), (0, QK_PAD - nqk))).astype(BF16)
    b_gate = jnp.pad(gla_b_gate, ((0, 0), (0, QK_PAD - nqk)))[:, None, :]
    return {
        "mix_w_in": w_in,
        "rg_conv_b": rg_conv_b[:, None, :],
        "rg_w_ri": w_ri,
        "rg_b_ri": b_ri,
        "rg_lambda": rg_lambda[:, None, :],
        "gla_w_gate": w_gate,
        "gla_b_gate": b_gate,
        "gla_norm_g": jnp.tile(gla_norm_g, (1, GLA_HEADS))[:, None, :],
        "head_ones": _block_ones(W_B, GLA_DV),
        "chunk_causal": _chunk_causal_mask(),
        "state_blockdiag": _state_blockdiag_mask(),
        "cv_dw_b": cv_dw_b[:, None, :],
        "cv_gn_g": cv_gn_g[:, None, :],
        "cv_gn_b": cv_gn_b[:, None, :],
        "group_ones": _block_ones(W_C, W_C // CONV_GROUPS),
    }


def kernel(x, mem, ffn1_w_in, ffn1_w_out, ln1_g, ln1_b, mix_w_in, rg_conv_w, rg_conv_b, rg_w_r, rg_b_r, rg_w_i, rg_b_i, rg_lambda, gla_w_gate, gla_b_gate, gla_norm_g, cv_dw_w, cv_dw_b, cv_gn_g, cv_gn_b, mix_w_out, ln2_g, ln2_b, xa_w_q, xa_w_kv, xa_w_o, ln3_g, ln3_b, ffn2_w_in, ffn2_w_out, ln4_g, ln4_b):
    bsz, s, dm = x.shape
    p = _prepare_params(mix_w_in, rg_conv_b, rg_w_r, rg_b_r, rg_w_i, rg_b_i, rg_lambda,
                        gla_w_gate, gla_b_gate, gla_norm_g, cv_dw_b, cv_gn_g, cv_gn_b)
    p.update({
        "rg_conv_w": rg_conv_w, "cv_dw_w": cv_dw_w,
        "mix_w_out": mix_w_out.astype(BF16),
        "ln2_g": ln2_g[:, None, :], "ln2_b": ln2_b[:, None, :],
    })
    ln1g, ln1b = ln1_g[:, None, :], ln1_b[:, None, :]
    ln3g, ln3b = ln3_g[:, None, :], ln3_b[:, None, :]
    ln4g, ln4b = ln4_g[:, None, :], ln4_b[:, None, :]

    nt = s // MIX_TM
    x = x.reshape(bsz, nt, SUBLANES, MIX_G, dm).swapaxes(2, 3).reshape(bsz, s, dm)
    x = x.reshape(bsz * s, dm)
    k4d, v4d = _kv_proj(mem, xa_w_kv)
    for l in range(DEPTH):
        x = _ffn(x, ffn1_w_in, ffn1_w_out, ln1g, ln1b, l)
        x = _mixer(x, p, l)
        x = _xattn(x, k4d, v4d, xa_w_q, xa_w_o, ln3g, ln3b, l)
        x = _ffn(x, ffn2_w_in, ffn2_w_out, ln4g, ln4b, l)
    return x.reshape(bsz, nt, MIX_G, SUBLANES, dm).swapaxes(2, 3).reshape(bsz, s, dm)
```

```python
import numpy as np
import jax
import jax.numpy as jnp
from jax import lax
from jax.experimental import pallas as pl
from jax.experimental.pallas import tpu as pltpu

D_MODEL = 1024
BATCH = 8
SEQ = 2048
DEPTH = 4
MEM_LEN = 256
D_FF = 2816
W_A = 384
W_B = 384
W_C = 256
RG_BLOCK = 64
RG_HEADS = W_A // RG_BLOCK
RG_CONV_K = 4
RG_C = 8.0
GLA_HEADS = 4
GLA_DV = W_B // GLA_HEADS
GLA_DK = GLA_DV // 2
GLA_RANK = 16
GLA_TAU = 16.0
GLA_CHUNK = 64
CONV_K = 31
CONV_GROUPS = 4
XA_HEADS = 4
XA_HEAD_DIM = D_MODEL // XA_HEADS
ALPHA = (2.0 * DEPTH) ** 0.25
EPS = 1e-5

BF16 = jnp.bfloat16
F32 = jnp.float32

SUBLANES = 8
LANES = 128

QK_PAD = 256
LR_PAD = 128
COL_RG = 0
COL_GLA = 2 * W_A
N_GLA = 2 * QK_PAD + 2 * W_B + LR_PAD
COL_CV = COL_GLA + N_GLA
N_IN_PAD = COL_CV + 2 * W_C

FFN_TM = 512
FFN_FC = 256
MIX_TM = 256
MIX_G = MIX_TM // SUBLANES
MIX_NCH = MIX_TM // GLA_CHUNK
SEG_PER_CHUNK = GLA_CHUNK // MIX_G
XA_TM = 1024
LN_ROWS = 32
CV_TAPS_PER_STAGE = 4
MIX_STAGE_ORDER = ("cv", "ln", "rg", "ln", "cv", "ln", "gla", "ln", "ln", "cv", "ln", "cv", "ln",
                   "rg", "ln", "cv", "gla", "cv", "gla", "cv", "gla", "rg", "cv", "gla", "gla",
                   "rg", "cv", "gla", "rg", "gla", "cv", "gla", "rg", "cv", "gla")
VMEM_LIMIT = 56 * 1024 * 1024

assert GLA_CHUNK % MIX_G == 0 and CONV_K - 1 <= MIX_G


def _dot(a, b):
    return jnp.dot(a, b, preferred_element_type=F32)


def _dot_nt(a, b):
    return lax.dot_general(a, b, (((1,), (1,)), ((), ())), preferred_element_type=F32)


def _dot_tn(a, b):
    return lax.dot_general(a, b, (((0,), (0,)), ((), ())), preferred_element_type=F32)


def _sigmoid(x):
    return 1.0 / (1.0 + jnp.exp(-x))


def _softplus(x):
    return jnp.maximum(x, 0.0) + jnp.log1p(jnp.exp(-jnp.abs(x)))


def _neg_expm1_of_2log(a, log_a):
    x2 = 2.0 * log_a
    sq = a * a
    sqm1 = sq - 1.0
    em1 = jnp.where(sq == 1.0, x2, sqm1 * x2 / jnp.log(sq))
    return -jnp.where(sqm1 == -1.0, -1.0, em1)


def _layer_norm(z, g, b):
    mu = jnp.mean(z, axis=-1, keepdims=True)
    d = z - mu
    var = jnp.mean(d * d, axis=-1, keepdims=True)
    return d * lax.rsqrt(var + EPS) * g + b


def _group_sum(y, ones_bf16):
    hi = y.astype(BF16)
    lo = (y - hi.astype(F32)).astype(BF16)
    return _dot(hi, ones_bf16) + _dot(lo, ones_bf16)


def _rows(v, g, n=1):
    return v[g * SUBLANES:(g + n) * SUBLANES, :]


def _zero_token(v):
    bits = lax.bitcast_convert_type(v, jnp.int32)
    acc = bits[0:SUBLANES, :]
    for r in range(1, v.shape[0] // SUBLANES):
        acc = acc | bits[r * SUBLANES:(r + 1) * SUBLANES, :]
    tok = acc[:, 0:LANES]
    for j in range(1, v.shape[1] // LANES):
        tok = tok | acc[:, j * LANES:(j + 1) * LANES]
    return lax.shift_right_logical(lax.shift_right_logical(tok, 16), 16)


def _tie(v, tokens):
    if not tokens:
        return v
    tok = tokens[0]
    for t in tokens[1:]:
        tok = tok | t
    head = v[0:SUBLANES, 0:LANES] + lax.bitcast_convert_type(tok, F32)
    top = jnp.concatenate([head, v[0:SUBLANES, LANES:]], axis=1)
    return jnp.concatenate([top, v[SUBLANES:, :]], axis=0)


def _lagged_layer_norm(z_ref, g_ref, b_ref, o_ref, tm):
    def stage(blk):
        rows = slice(blk * LN_ROWS, (blk + 1) * LN_ROWS)
        out = _layer_norm(z_ref[rows, :], g_ref[...], b_ref[...])
        o_ref[rows, :] = out
        return _zero_token(out)
    return [lambda blk=blk: stage(blk) for blk in range(tm // LN_ROWS)]


def _ffn_kernel(x_ref, win_ref, wout_ref, g_ref, b_ref, o_ref, act_ref, z_ref):
    i = pl.program_id(0)
    n_tiles = pl.num_programs(0) - 1

    @pl.when(i == 0)
    def _():
        z_ref[...] = jnp.zeros_like(z_ref)

    @pl.when(i < n_tiles)
    def _():
        x = x_ref[...]
        xb = x.astype(BF16)
        ln_stages = _lagged_layer_norm(z_ref, g_ref, b_ref, o_ref, FFN_TM)
        n_chunks = D_FF // FFN_FC
        done = 0
        tokens = []
        for c in range(n_chunks):
            w_gate = win_ref[:, c * FFN_FC:(c + 1) * FFN_FC].astype(BF16)
            w_up = win_ref[:, D_FF + c * FFN_FC:D_FF + (c + 1) * FFN_FC].astype(BF16)
            gate = _tie(_dot(xb, w_gate), tokens)
            up = _dot(xb, w_up)
            act_ref[:, c * FFN_FC:(c + 1) * FFN_FC] = (gate * _sigmoid(gate) * up).astype(BF16)
            upto = min(len(ln_stages), ((c + 1) * len(ln_stages)) // (n_chunks - 1))
            tokens = [st() for st in ln_stages[done:upto]]
            done = upto
        y = _dot(act_ref[...], wout_ref[...].astype(BF16))
        z_ref[...] = ALPHA * x + 0.5 * y

    @pl.when(i == n_tiles)
    def _():
        o_ref[...] = _layer_norm(z_ref[...], g_ref[...], b_ref[...])


def _const_spec(shape, layer):
    nd = len(shape)
    return pl.BlockSpec((None,) + tuple(shape), lambda *_: (layer,) + (0,) * nd,
                        pipeline_mode=pl.Buffered(1))


def _shared_spec(shape):
    nd = len(shape)
    return pl.BlockSpec(tuple(shape), lambda *_: (0,) * nd, pipeline_mode=pl.Buffered(1))


def _ffn(x2d, w_in, w_out, g, b, layer):
    t = x2d.shape[0]
    n_tiles = t // FFN_TM
    return pl.pallas_call(
        _ffn_kernel,
        grid=(n_tiles + 1,),
        in_specs=[
            pl.BlockSpec((FFN_TM, D_MODEL), lambda i: (jnp.minimum(i, n_tiles - 1), 0)),
            _const_spec((D_MODEL, 2 * D_FF), layer),
            _const_spec((D_FF, D_MODEL), layer),
            _const_spec((1, D_MODEL), layer),
            _const_spec((1, D_MODEL), layer),
        ],
        out_specs=pl.BlockSpec((FFN_TM, D_MODEL), lambda i: (jnp.maximum(i - 1, 0), 0)),
        out_shape=jax.ShapeDtypeStruct((t, D_MODEL), F32),
        scratch_shapes=[pltpu.VMEM((FFN_TM, D_FF), BF16), pltpu.VMEM((FFN_TM, D_MODEL), F32)],
        compiler_params=pltpu.CompilerParams(
            dimension_semantics=("arbitrary",), vmem_limit_bytes=VMEM_LIMIT),
        name="ffn_ln",
    )(x2d, w_in, w_out, g, b)


def _shifted_history(cur, hist_ref, n_hist):
    tail = _rows(cur, MIX_G - n_hist, n_hist)
    sub = lax.broadcasted_iota(jnp.int32, tail.shape, 0) & (SUBLANES - 1)
    merged = jnp.where(sub == SUBLANES - 1, hist_ref[...], tail)
    hist_ref[...] = tail
    return [pltpu.roll(_rows(merged, i), 1, 0) for i in range(n_hist)]


def _causal_conv(cur, hist_ref, w_ref, b_ref, k_taps):
    ext = jnp.concatenate(_shifted_history(cur, hist_ref, k_taps - 1) + [cur], axis=0)
    acc = b_ref[...]
    for k in range(k_taps):
        acc = acc + w_ref[k:k + 1, :] * _rows(ext, k, MIX_G)
    return acc


def _linear_recurrence(a, u, h_ref):
    hs, ps = [_rows(u, 0)], [_rows(a, 0)]
    for g in range(1, MIX_G):
        ag = _rows(a, g)
        hs.append(ag * hs[-1] + _rows(u, g))
        ps.append(ag * ps[-1])
    hc, pc = hs[-1], ps[-1]
    sub = lax.broadcasted_iota(jnp.int32, hc.shape, 0)
    d = 1
    while d < SUBLANES:
        keep = sub >= d
        h_s = jnp.where(keep, pltpu.roll(hc, d, 0), 0.0)
        p_s = jnp.where(keep, pltpu.roll(pc, d, 0), 1.0)
        hc = pc * h_s + hc
        pc = pc * p_s
        d *= 2
    h_in = h_ref[...]
    seg_end = hc + pc * h_in
    carry = jnp.where(sub == 0, h_in, pltpu.roll(seg_end, 1, 0))
    h_ref[...] = jnp.broadcast_to(seg_end[SUBLANES - 1:SUBLANES, :], h_in.shape)
    return jnp.concatenate([h + p * carry for h, p in zip(hs, ps)], axis=0)


def _mixer_kernel(x_ref, win_ref, rgcw_ref, rgcb_ref, wri_ref, bri_ref, lam_ref,
                  wgate_ref, bgate_ref, ng_ref, hones_ref, cmask_ref, bdmask_ref,
                  cvw_ref, cvb_ref, gng_ref, gnb_ref, gones_ref,
                  wout_ref, lng_ref, lnb_ref,
                  o_ref,
                  mix_ref, hista_ref, histc_ref, h_ref, st_ref, z_ref):
    tm = MIX_TM
    i = pl.program_id(0)
    tile = jnp.minimum(i, pl.num_programs(0) - 2)

    @pl.when(i == 0)
    def _():
        z_ref[...] = jnp.zeros_like(z_ref)

    @pl.when(lax.rem(tile, SEQ // MIX_TM) == 0)
    def _():
        hista_ref[...] = jnp.zeros_like(hista_ref)
        histc_ref[...] = jnp.zeros_like(histc_ref)
        h_ref[...] = jnp.zeros_like(h_ref)
        st_ref[...] = jnp.zeros_like(st_ref)

    x = x_ref[...]
    xb = x.astype(BF16)

    def rg_group():
        xy = _dot(xb, win_ref[:, COL_RG:COL_RG + 2 * W_A])
        yield
        ya = xy[:, W_A:]
        xc = _causal_conv(xy[:, :W_A], hista_ref, rgcw_ref, rgcb_ref, RG_CONV_K)
        ri = _dot(xc.astype(BF16), wri_ref[...]) + bri_ref[...]
        yield
        r = _sigmoid(ri[:, :W_A])
        ig = _sigmoid(ri[:, W_A:])
        yield
        log_a = (-RG_C * r) * _softplus(-lam_ref[...])
        a = jnp.exp(log_a)
        u = jnp.sqrt(_neg_expm1_of_2log(a, log_a)) * (ig * xc)
        yield
        h = _linear_recurrence(a, u, h_ref)
        yield
        gelu = 0.5 * ya * (1.0 + jnp.tanh(0.7978845608028654 * (ya + 0.044715 * (ya * ya * ya))))
        mix_ref[:, 0:W_A] = (h * gelu).astype(BF16)

    def gla_group():
        gl = _dot(xb, win_ref[:, COL_GLA:COL_GLA + N_GLA])
        yield
        q = gl[:, 0:QK_PAD]
        kk = gl[:, QK_PAD:2 * QK_PAD]
        v = gl[:, 2 * QK_PAD:2 * QK_PAD + W_B]
        g = gl[:, 2 * QK_PAD + W_B:2 * QK_PAD + 2 * W_B]
        lr = gl[:, 2 * QK_PAD + 2 * W_B:]
        gate_pre = _dot(lr.astype(BF16), wgate_ref[...]) + bgate_ref[...]
        la = (jnp.minimum(gate_pre, 0.0) - jnp.log(1.0 + jnp.exp(-jnp.abs(gate_pre)))) * (1.0 / GLA_TAU)
        cs = [_rows(la, 0)]
        for gi in range(1, MIX_G):
            cs.append(cs[-1] + _rows(la, gi))
        sub = lax.broadcasted_iota(jnp.int32, (SUBLANES, QK_PAD), 0)
        seg_in_chunk = sub & (SEG_PER_CHUNK - 1)
        seg_tot = cs[-1]
        fix = jnp.zeros_like(seg_tot)
        for j in range(1, SEG_PER_CHUNK):
            fix = fix + jnp.where(seg_in_chunk >= j, pltpu.roll(seg_tot, j, 0), 0.0)
        bc = jnp.concatenate([c + fix for c in cs], axis=0)
        blast = cs[-1] + fix
        for j in range(1, SEG_PER_CHUNK):
            blast = jnp.where(seg_in_chunk == SEG_PER_CHUNK - 1 - j,
                              pltpu.roll(blast, SUBLANES - j, 0), blast)
        blast_t = jnp.concatenate([blast] * MIX_G, axis=0)
        decay = jnp.exp(blast)
        yield
        qd = (q * (GLA_DK ** -0.5)) * jnp.exp(bc)
        ki = kk * jnp.exp(-bc)
        kt = kk * jnp.exp(blast_t - bc)

        lane_q = lax.broadcasted_iota(jnp.int32, (SUBLANES, QK_PAD), 1)
        lane_v = lax.broadcasted_iota(jnp.int32, (SUBLANES, W_B), 1)
        head_q = [(lane_q >= hh * GLA_DK) & (lane_q < (hh + 1) * GLA_DK) for hh in range(GLA_HEADS)]
        head_v = [(lane_v >= hh * GLA_DV) & (lane_v < (hh + 1) * GLA_DV) for hh in range(GLA_HEADS)]
        in_chunk = [(sub >= c * SEG_PER_CHUNK) & (sub < (c + 1) * SEG_PER_CHUNK) for c in range(MIX_NCH)]

        def keep(val, m8):
            return jnp.concatenate([jnp.where(m8, _rows(val, gi), 0.0) for gi in range(MIX_G)], axis=0)

        ki_heads = jnp.concatenate([keep(ki, m) for m in head_q], axis=0).astype(BF16)
        yield
        s = _dot_nt(qd.astype(BF16), ki_heads)
        v_heads = jnp.concatenate([keep(v, m) for m in head_v], axis=0).astype(BF16)
        yield
        allowed = cmask_ref[...] > 0.5
        p = jnp.concatenate([jnp.where(allowed, s[:, hh * tm:(hh + 1) * tm], 0.0)
                             for hh in range(GLA_HEADS)], axis=1).astype(BF16)
        kt_chunks = jnp.concatenate([keep(kt, m) for m in in_chunk], axis=1).astype(BF16)
        yield
        o = _dot(p, v_heads)
        inc = _dot_tn(v.astype(BF16), kt_chunks)
        yield
        bd = bdmask_ref[...]
        st = st_ref[...]
        states = []
        for c in range(MIX_NCH):
            states.append(st.astype(BF16))
            row = (c + 1) * SEG_PER_CHUNK - 1
            st = st * decay[row:row + 1, :] + inc[:, c * QK_PAD:(c + 1) * QK_PAD] * bd
        st_ref[...] = st
        qd_chunks = jnp.concatenate([keep(qd, m) for m in in_chunk], axis=1).astype(BF16)
        yield
        o = o + _dot_nt(qd_chunks, jnp.concatenate(states, axis=1))
        yield
        ms = _group_sum(o * o, hones_ref[...]) * (1.0 / GLA_DV)
        yield
        o = o * lax.rsqrt(ms + EPS) * ng_ref[...]
        mix_ref[:, W_A:W_A + W_B] = (o * (g * _sigmoid(g))).astype(BF16)

    def conv_group():
        cc = _dot(xb, win_ref[:, COL_CV:COL_CV + 2 * W_C])
        yield
        cur = cc[:, :W_C] * _sigmoid(cc[:, W_C:])
        ext = jnp.concatenate(_shifted_history(cur, histc_ref, CONV_K - 1) + [cur], axis=0)
        acc = cvb_ref[...]
        for k in range(CONV_K):
            acc = acc + cvw_ref[k:k + 1, :] * _rows(ext, k, MIX_G)
            if k % CV_TAPS_PER_STAGE == CV_TAPS_PER_STAGE - 1:
                yield
        gsz = W_C // CONV_GROUPS
        mu = _group_sum(acc, gones_ref[...]) * (1.0 / gsz)
        yield
        dd = acc - mu
        var = _group_sum(dd * dd, gones_ref[...]) * (1.0 / gsz)
        yield
        un = dd * lax.rsqrt(var + EPS) * gng_ref[...] + gnb_ref[...]
        mix_ref[:, W_A + W_B:] = (un * _sigmoid(un)).astype(BF16)

    def lagged_ln():
        for st in _lagged_layer_norm(z_ref, lng_ref, lnb_ref, o_ref, tm):
            st()
            yield

    groups = {"rg": rg_group(), "gla": gla_group(), "cv": conv_group(), "ln": lagged_ln()}
    for name in MIX_STAGE_ORDER:
        next(groups[name], None)
    for gen in groups.values():
        for _ in gen:
            pass

    y = _dot(mix_ref[...], wout_ref[...])
    z_ref[...] = ALPHA * x + y


def _mixer(x2d, p, layer):
    t = x2d.shape[0]
    tm = MIX_TM
    n_tiles = t // tm
    return pl.pallas_call(
        _mixer_kernel,
        grid=(n_tiles + 1,),
        in_specs=[
            pl.BlockSpec((tm, D_MODEL), lambda i: (jnp.minimum(i, n_tiles - 1), 0)),
            _const_spec((D_MODEL, N_IN_PAD), layer),
            _const_spec((RG_CONV_K, W_A), layer),
            _const_spec((1, W_A), layer),
            _const_spec((W_A, 2 * W_A), layer),
            _const_spec((1, 2 * W_A), layer),
            _const_spec((1, W_A), layer),
            _const_spec((LR_PAD, QK_PAD), layer),
            _const_spec((1, QK_PAD), layer),
            _const_spec((1, W_B), layer),
            _shared_spec((W_B, W_B)),
            _shared_spec((tm, tm)),
            _shared_spec((W_B, QK_PAD)),
            _const_spec((CONV_K, W_C), layer),
            _const_spec((1, W_C), layer),
            _const_spec((1, W_C), layer),
            _const_spec((1, W_C), layer),
            _shared_spec((W_C, W_C)),
            _const_spec((D_MODEL, D_MODEL), layer),
            _const_spec((1, D_MODEL), layer),
            _const_spec((1, D_MODEL), layer),
        ],
        out_specs=pl.BlockSpec((tm, D_MODEL), lambda i: (jnp.maximum(i - 1, 0), 0)),
        out_shape=jax.ShapeDtypeStruct((t, D_MODEL), F32),
        scratch_shapes=[
            pltpu.VMEM((tm, D_MODEL), BF16),
            pltpu.VMEM(((RG_CONV_K - 1) * SUBLANES, W_A), F32),
            pltpu.VMEM(((CONV_K - 1) * SUBLANES, W_C), F32),
            pltpu.VMEM((SUBLANES, W_A), F32),
            pltpu.VMEM((W_B, QK_PAD), F32),
            pltpu.VMEM((tm, D_MODEL), F32),
        ],
        compiler_params=pltpu.CompilerParams(
            dimension_semantics=("arbitrary",), vmem_limit_bytes=VMEM_LIMIT),
        name="mixer_ln",
    )(x2d, p["mix_w_in"], p["rg_conv_w"], p["rg_conv_b"], p["rg_w_ri"], p["rg_b_ri"],
      p["rg_lambda"], p["gla_w_gate"], p["gla_b_gate"], p["gla_norm_g"], p["head_ones"],
      p["chunk_causal"], p["state_blockdiag"],
      p["cv_dw_w"], p["cv_dw_b"], p["cv_gn_g"], p["cv_gn_b"], p["group_ones"],
      p["mix_w_out"], p["ln2_g"], p["ln2_b"])


def _kv_kernel(m_ref, w_ref, k_ref, v_ref):
    kv = _dot(m_ref[...].astype(BF16), w_ref[...].astype(BF16))
    k_ref[...] = kv[:, :D_MODEL].astype(BF16)
    v_ref[...] = kv[:, D_MODEL:].astype(BF16)


def _kv_proj(mem3d, w_kv):
    bsz, m, _ = mem3d.shape
    depth = w_kv.shape[0]
    ospec = pl.BlockSpec((None, bsz * m, D_MODEL), lambda l: (l, 0, 0))
    k, v = pl.pallas_call(
        _kv_kernel,
        grid=(depth,),
        in_specs=[pl.BlockSpec((bsz * m, D_MODEL), lambda l: (0, 0)),
                  pl.BlockSpec((None, D_MODEL, 2 * D_MODEL), lambda l: (l, 0, 0))],
        out_specs=[ospec, ospec],
        out_shape=[jax.ShapeDtypeStruct((depth, bsz * m, D_MODEL), BF16)] * 2,
        compiler_params=pltpu.CompilerParams(
            dimension_semantics=("arbitrary",), vmem_limit_bytes=VMEM_LIMIT),
        name="xattn_kv",
    )(mem3d.reshape(bsz * m, D_MODEL), w_kv)
    return k.reshape(depth, bsz, m, D_MODEL), v.reshape(depth, bsz, m, D_MODEL)


def _xattn_kernel(x_ref, k_ref, v_ref, wq_ref, wo_ref, g_ref, b_ref, o_ref, att_ref, z_ref):
    i = pl.program_id(0)
    n_tiles = pl.num_programs(0) - 1

    @pl.when(i == 0)
    def _():
        z_ref[...] = jnp.zeros_like(z_ref)

    @pl.when(i < n_tiles)
    def _():
        x = x_ref[...]
        ln_stages = _lagged_layer_norm(z_ref, g_ref, b_ref, o_ref, XA_TM)
        per_phase = len(ln_stages) // XA_HEADS
        q = _dot(x.astype(BF16), wq_ref[...].astype(BF16)).astype(BF16)
        for hh in range(XA_HEADS):
            tokens = [st() for st in ln_stages[hh * per_phase:(hh + 1) * per_phase]]
            c0 = hh * XA_HEAD_DIM
            s = _dot_nt(q[:, c0:c0 + XA_HEAD_DIM], k_ref[:, c0:c0 + XA_HEAD_DIM]) * (XA_HEAD_DIM ** -0.5)
            s = _tie(s, tokens)
            e = jnp.exp(s - jnp.max(s, axis=-1, keepdims=True))
            p = e / jnp.sum(e, axis=-1, keepdims=True)
            att_ref[:, c0:c0 + XA_HEAD_DIM] = _dot(p.astype(BF16), v_ref[:, c0:c0 + XA_HEAD_DIM]).astype(BF16)
        y = _dot(att_ref[...], wo_ref[...].astype(BF16))
        z_ref[...] = ALPHA * x + y

    @pl.when(i == n_tiles)
    def _():
        o_ref[...] = _layer_norm(z_ref[...], g_ref[...], b_ref[...])


def _xattn(x2d, k4d, v4d, w_q, w_o, g, b, layer):
    t = x2d.shape[0]
    _, bsz, m, _ = k4d.shape
    tm = XA_TM
    n_tiles = t // tm
    tiles_per_batch = n_tiles // bsz
    mspec = pl.BlockSpec((None, None, m, D_MODEL),
                         lambda i: (layer, jnp.minimum(i, n_tiles - 1) // tiles_per_batch, 0, 0))
    return pl.pallas_call(
        _xattn_kernel,
        grid=(n_tiles + 1,),
        in_specs=[
            pl.BlockSpec((tm, D_MODEL), lambda i: (jnp.minimum(i, n_tiles - 1), 0)),
            mspec, mspec,
            _const_spec((D_MODEL, D_MODEL), layer),
            _const_spec((D_MODEL, D_MODEL), layer),
            _const_spec((1, D_MODEL), layer),
            _const_spec((1, D_MODEL), layer),
        ],
        out_specs=pl.BlockSpec((tm, D_MODEL), lambda i: (jnp.maximum(i - 1, 0), 0)),
        out_shape=jax.ShapeDtypeStruct((t, D_MODEL), F32),
        scratch_shapes=[pltpu.VMEM((tm, D_MODEL), BF16), pltpu.VMEM((tm, D_MODEL), F32)],
        compiler_params=pltpu.CompilerParams(
            dimension_semantics=("arbitrary",), vmem_limit_bytes=VMEM_LIMIT),
        name="xattn_ln",
    )(x2d, k4d, v4d, w_q, w_o, g, b)


def _block_ones(n, group):
    idx = np.arange(n) // group
    return jnp.asarray((idx[:, None] == idx[None, :]).astype(np.float32), dtype=BF16)


def _chunk_causal_mask():
    row = np.arange(MIX_TM)
    step = (row % SUBLANES) * MIX_G + row // SUBLANES
    chunk = step // GLA_CHUNK
    ok = (chunk[:, None] == chunk[None, :]) & (step[:, None] >= step[None, :])
    return jnp.asarray(ok.astype(np.float32))


def _state_blockdiag_mask():
    vh = np.arange(W_B) // GLA_DV
    kh = np.arange(QK_PAD) // GLA_DK
    return jnp.asarray((vh[:, None] == kh[None, :]).astype(np.float32))


NQK = GLA_HEADS * GLA_DK
SRC_Q = 2 * W_A
SRC_K = SRC_Q + NQK
SRC_V = SRC_K + NQK
SRC_LR = SRC_V + 2 * W_B
SRC_CV = SRC_LR + GLA_RANK
N_IN_SRC = SRC_CV + 2 * W_C
RELAYOUT_ROWS = 256


def _relayout_kernel(w_ref, o_ref):
    w = w_ref[...]

    def zeros(n):
        return jnp.zeros((w.shape[0], n), w.dtype)

    o_ref[...] = jnp.concatenate([
        w[:, :SRC_Q],
        w[:, SRC_Q:SRC_K], zeros(QK_PAD - NQK),
        w[:, SRC_K:SRC_V], zeros(QK_PAD - NQK),
        w[:, SRC_V:SRC_LR],
        w[:, SRC_LR:SRC_CV], zeros(LR_PAD - GLA_RANK),
        w[:, SRC_CV:],
    ], axis=-1).astype(BF16)


def _relayout_mix_w_in(mix_w_in):
    depth = mix_w_in.shape[0]
    return pl.pallas_call(
        _relayout_kernel,
        grid=(depth, D_MODEL // RELAYOUT_ROWS),
        in_specs=[pl.BlockSpec((None, RELAYOUT_ROWS, N_IN_SRC), lambda l, r: (l, r, 0))],
        out_specs=pl.BlockSpec((None, RELAYOUT_ROWS, N_IN_PAD), lambda l, r: (l, r, 0)),
        out_shape=jax.ShapeDtypeStruct((depth, D_MODEL, N_IN_PAD), BF16),
        compiler_params=pltpu.CompilerParams(
            dimension_semantics=("arbitrary", "arbitrary"), vmem_limit_bytes=VMEM_LIMIT),
        name="mix_w_in_relayout",
    )(mix_w_in)


def _prepare_params(mix_w_in, rg_conv_b, rg_w_r, rg_b_r, rg_w_i, rg_b_i, rg_lambda,
                    gla_w_gate, gla_b_gate, gla_norm_g, cv_dw_b, cv_gn_g, cv_gn_b):
    L = DEPTH
    nqk = NQK
    w_in = _relayout_mix_w_in(mix_w_in)

    eye = jnp.eye(RG_HEADS, dtype=rg_w_r.dtype)

    def blockdiag(w):
        return jnp.einsum("lhij,hg->lhigj", w, eye).reshape(L, W_A, W_A)

    w_ri = jnp.concatenate([blockdiag(rg_w_r), blockdiag(rg_w_i)], axis=-1).astype(BF16)
    b_ri = jnp.concatenate([rg_b_r, rg_b_i], axis=-1)[:, None, :]
    w_gate = jnp.pad(gla_w_gate, ((0, 0), (0, LR_PAD - GLA_RANK), (0, QK_PAD - nqk))).astype(BF16)
    b_gate = jnp.pad(gla_b_gate, ((0, 0), (0, QK_PAD - nqk)))[:, None, :]
    return {
        "mix_w_in": w_in,
        "rg_conv_b": rg_conv_b[:, None, :],
        "rg_w_ri": w_ri,
        "rg_b_ri": b_ri,
        "rg_lambda": rg_lambda[:, None, :],
        "gla_w_gate": w_gate,
        "gla_b_gate": b_gate,
        "gla_norm_g": jnp.tile(gla_norm_g, (1, GLA_HEADS))[:, None, :],
        "head_ones": _block_ones(W_B, GLA_DV),
        "chunk_causal": _chunk_causal_mask(),
        "state_blockdiag": _state_blockdiag_mask(),
        "cv_dw_b": cv_dw_b[:, None, :],
        "cv_gn_g": cv_gn_g[:, None, :],
        "cv_gn_b": cv_gn_b[:, None, :],
        "group_ones": _block_ones(W_C, W_C // CONV_GROUPS),
    }


def kernel(x, mem, ffn1_w_in, ffn1_w_out, ln1_g, ln1_b, mix_w_in, rg_conv_w, rg_conv_b, rg_w_r, rg_b_r, rg_w_i, rg_b_i, rg_lambda, gla_w_gate, gla_b_gate, gla_norm_g, cv_dw_w, cv_dw_b, cv_gn_g, cv_gn_b, mix_w_out, ln2_g, ln2_b, xa_w_q, xa_w_kv, xa_w_o, ln3_g, ln3_b, ffn2_w_in, ffn2_w_out, ln4_g, ln4_b):
    bsz, s, dm = x.shape
    p = _prepare_params(mix_w_in, rg_conv_b, rg_w_r, rg_b_r, rg_w_i, rg_b_i, rg_lambda,
                        gla_w_gate, gla_b_gate, gla_norm_g, cv_dw_b, cv_gn_g, cv_gn_b)
    p.update({
        "rg_conv_w": rg_conv_w, "cv_dw_w": cv_dw_w,
        "mix_w_out": mix_w_out.astype(BF16),
        "ln2_g": ln2_g[:, None, :], "ln2_b": ln2_b[:, None, :],
    })
    ln1g, ln1b = ln1_g[:, None, :], ln1_b[:, None, :]
    ln3g, ln3b = ln3_g[:, None, :], ln3_b[:, None, :]
    ln4g, ln4b = ln4_g[:, None, :], ln4_b[:, None, :]

    nt = s // MIX_TM
    x = x.reshape(bsz, nt, SUBLANES, MIX_G, dm).swapaxes(2, 3).reshape(bsz, s, dm)
    x = x.reshape(bsz * s, dm)
    k4d, v4d = _kv_proj(mem, xa_w_kv)
    for l in range(DEPTH):
        x = _ffn(x, ffn1_w_in, ffn1_w_out, ln1g, ln1b, l)
        x = _mixer(x, p, l)
        x = _xattn(x, k4d, v4d, xa_w_q, xa_w_o, ln3g, ln3b, l)
        x = _ffn(x, ffn2_w_in, ffn2_w_out, ln4g, ln4b, l)
    return x.reshape(bsz, nt, MIX_G, SUBLANES, dm).swapaxes(2, 3).reshape(bsz, s, dm)
```

```python
import numpy as np
import jax
import jax.numpy as jnp
from jax import lax
from jax.experimental import pallas as pl
from jax.experimental.pallas import tpu as pltpu

D_MODEL = 1024
BATCH = 8
SEQ = 2048
DEPTH = 4
MEM_LEN = 256
D_FF = 2816
W_A = 384
W_B = 384
W_C = 256
RG_BLOCK = 64
RG_HEADS = W_A // RG_BLOCK
RG_CONV_K = 4
RG_C = 8.0
GLA_HEADS = 4
GLA_DV = W_B // GLA_HEADS
GLA_DK = GLA_DV // 2
GLA_RANK = 16
GLA_TAU = 16.0
GLA_CHUNK = 64
CONV_K = 31
CONV_GROUPS = 4
XA_HEADS = 4
XA_HEAD_DIM = D_MODEL // XA_HEADS
ALPHA = (2.0 * DEPTH) ** 0.25
EPS = 1e-5

BF16 = jnp.bfloat16
F32 = jnp.float32

SUBLANES = 8
LANES = 128

QK_PAD = 256
LR_PAD = 128
COL_RG = 0
COL_GLA = 2 * W_A
N_GLA = 2 * QK_PAD + 2 * W_B + LR_PAD
COL_CV = COL_GLA + N_GLA
N_IN_PAD = COL_CV + 2 * W_C

FFN_TM = 512
FFN_FC = 256
MIX_TM = 256
MIX_G = MIX_TM // SUBLANES
MIX_NCH = MIX_TM // GLA_CHUNK
SEG_PER_CHUNK = GLA_CHUNK // MIX_G
XA_TM = 1024
LN_ROWS = 32
CV_TAPS_PER_STAGE = 4
MIX_STAGE_ORDER = ("cv", "ln", "rg", "ln", "cv", "ln", "gla", "ln", "ln", "cv", "ln", "cv", "ln",
                   "rg", "ln", "cv", "gla", "cv", "gla", "cv", "gla", "rg", "cv", "gla", "gla",
                   "rg", "cv", "gla", "rg", "gla", "cv", "gla", "rg", "cv", "gla")
VMEM_LIMIT = 56 * 1024 * 1024

assert GLA_CHUNK % MIX_G == 0 and CONV_K - 1 <= MIX_G


def _dot(a, b):
    return jnp.dot(a, b, preferred_element_type=F32)


def _dot_nt(a, b):
    return lax.dot_general(a, b, (((1,), (1,)), ((), ())), preferred_element_type=F32)


def _dot_tn(a, b):
    return lax.dot_general(a, b, (((0,), (0,)), ((), ())), preferred_element_type=F32)


def _sigmoid(x):
    return 1.0 / (1.0 + jnp.exp(-x))


def _softplus(x):
    return jnp.maximum(x, 0.0) + jnp.log1p(jnp.exp(-jnp.abs(x)))


def _neg_expm1_of_2log(a, log_a):
    x2 = 2.0 * log_a
    sq = a * a
    sqm1 = sq - 1.0
    em1 = jnp.where(sq == 1.0, x2, sqm1 * x2 / jnp.log(sq))
    return -jnp.where(sqm1 == -1.0, -1.0, em1)


def _layer_norm(z, g, b):
    mu = jnp.mean(z, axis=-1, keepdims=True)
    d = z - mu
    var = jnp.mean(d * d, axis=-1, keepdims=True)
    return d * lax.rsqrt(var + EPS) * g + b


def _group_sum(y, ones_bf16):
    hi = y.astype(BF16)
    lo = (y - hi.astype(F32)).astype(BF16)
    return _dot(hi, ones_bf16) + _dot(lo, ones_bf16)


def _rows(v, g, n=1):
    return v[g * SUBLANES:(g + n) * SUBLANES, :]


def _zero_token(v):
    bits = lax.bitcast_convert_type(v, jnp.int32)
    acc = bits[0:SUBLANES, :]
    for r in range(1, v.shape[0] // SUBLANES):
        acc = acc | bits[r * SUBLANES:(r + 1) * SUBLANES, :]
    tok = acc[:, 0:LANES]
    for j in range(1, v.shape[1] // LANES):
        tok = tok | acc[:, j * LANES:(j + 1) * LANES]
    return lax.shift_right_logical(lax.shift_right_logical(tok, 16), 16)


def _tie(v, tokens):
    if not tokens:
        return v
    tok = tokens[0]
    for t in tokens[1:]:
        tok = tok | t
    head = v[0:SUBLANES, 0:LANES] + lax.bitcast_convert_type(tok, F32)
    top = jnp.concatenate([head, v[0:SUBLANES, LANES:]], axis=1)
    return jnp.concatenate([top, v[SUBLANES:, :]], axis=0)


def _lagged_layer_norm(z_ref, g_ref, b_ref, o_ref, tm):
    def stage(blk):
        rows = slice(blk * LN_ROWS, (blk + 1) * LN_ROWS)
        out = _layer_norm(z_ref[rows, :], g_ref[...], b_ref[...])
        o_ref[rows, :] = out
        return _zero_token(out)
    return [lambda blk=blk: stage(blk) for blk in range(tm // LN_ROWS)]


def _ffn_kernel(x_ref, win_ref, wout_ref, g_ref, b_ref, o_ref, act_ref, z_ref):
    i = pl.program_id(0)
    n_tiles = pl.num_programs(0) - 1

    @pl.when(i == 0)
    def _():
        z_ref[...] = jnp.zeros_like(z_ref)

    @pl.when(i < n_tiles)
    def _():
        x = x_ref[...]
        xb = x.astype(BF16)
        ln_stages = _lagged_layer_norm(z_ref, g_ref, b_ref, o_ref, FFN_TM)
        n_chunks = D_FF // FFN_FC
        done = 0
        tokens = []
        for c in range(n_chunks):
            w_gate = win_ref[:, c * FFN_FC:(c + 1) * FFN_FC].astype(BF16)
            w_up = win_ref[:, D_FF + c * FFN_FC:D_FF + (c + 1) * FFN_FC].astype(BF16)
            gate = _tie(_dot(xb, w_gate), tokens)
            up = _dot(xb, w_up)
            act_ref[:, c * FFN_FC:(c + 1) * FFN_FC] = (gate * _sigmoid(gate) * up).astype(BF16)
            upto = min(len(ln_stages), ((c + 1) * len(ln_stages)) // (n_chunks - 1))
            tokens = [st() for st in ln_stages[done:upto]]
            done = upto
        y = _dot(act_ref[...], wout_ref[...].astype(BF16))
        z_ref[...] = ALPHA * x + 0.5 * y

    @pl.when(i == n_tiles)
    def _():
        o_ref[...] = _layer_norm(z_ref[...], g_ref[...], b_ref[...])


def _const_spec(shape, layer):
    nd = len(shape)
    return pl.BlockSpec((None,) + tuple(shape), lambda *_: (layer,) + (0,) * nd,
                        pipeline_mode=pl.Buffered(1))


def _shared_spec(shape):
    nd = len(shape)
    return pl.BlockSpec(tuple(shape), lambda *_: (0,) * nd, pipeline_mode=pl.Buffered(1))


def _ffn(x2d, w_in, w_out, g, b, layer):
    t = x2d.shape[0]
    n_tiles = t // FFN_TM
    return pl.pallas_call(
        _ffn_kernel,
        grid=(n_tiles + 1,),
        in_specs=[
            pl.BlockSpec((FFN_TM, D_MODEL), lambda i: (jnp.minimum(i, n_tiles - 1), 0)),
            _const_spec((D_MODEL, 2 * D_FF), layer),
            _const_spec((D_FF, D_MODEL), layer),
            _const_spec((1, D_MODEL), layer),
            _const_spec((1, D_MODEL), layer),
        ],
        out_specs=pl.BlockSpec((FFN_TM, D_MODEL), lambda i: (jnp.maximum(i - 1, 0), 0)),
        out_shape=jax.ShapeDtypeStruct((t, D_MODEL), F32),
        scratch_shapes=[pltpu.VMEM((FFN_TM, D_FF), BF16), pltpu.VMEM((FFN_TM, D_MODEL), F32)],
        compiler_params=pltpu.CompilerParams(
            dimension_semantics=("arbitrary",), vmem_limit_bytes=VMEM_LIMIT),
        name="ffn_ln",
    )(x2d, w_in, w_out, g, b)


def _shifted_history(cur, hist_ref, n_hist):
    tail = _rows(cur, MIX_G - n_hist, n_hist)
    sub = lax.broadcasted_iota(jnp.int32, tail.shape, 0) & (SUBLANES - 1)
    merged = jnp.where(sub == SUBLANES - 1, hist_ref[...], tail)
    hist_ref[...] = tail
    return [pltpu.roll(_rows(merged, i), 1, 0) for i in range(n_hist)]


def _causal_conv(cur, hist_ref, w_ref, b_ref, k_taps):
    ext = jnp.concatenate(_shifted_history(cur, hist_ref, k_taps - 1) + [cur], axis=0)
    acc = b_ref[...]
    for k in range(k_taps):
        acc = acc + w_ref[k:k + 1, :] * _rows(ext, k, MIX_G)
    return acc


def _linear_recurrence(a, u, h_ref):
    hs, ps = [_rows(u, 0)], [_rows(a, 0)]
    for g in range(1, MIX_G):
        ag = _rows(a, g)
        hs.append(ag * hs[-1] + _rows(u, g))
        ps.append(ag * ps[-1])
    hc, pc = hs[-1], ps[-1]
    sub = lax.broadcasted_iota(jnp.int32, hc.shape, 0)
    d = 1
    while d < SUBLANES:
        keep = sub >= d
        h_s = jnp.where(keep, pltpu.roll(hc, d, 0), 0.0)
        p_s = jnp.where(keep, pltpu.roll(pc, d, 0), 1.0)
        hc = pc * h_s + hc
        pc = pc * p_s
        d *= 2
    h_in = h_ref[...]
    seg_end = hc + pc * h_in
    carry = jnp.where(sub == 0, h_in, pltpu.roll(seg_end, 1, 0))
    h_ref[...] = jnp.broadcast_to(seg_end[SUBLANES - 1:SUBLANES, :], h_in.shape)
    return jnp.concatenate([h + p * carry for h, p in zip(hs, ps)], axis=0)


def _mixer_kernel(x_ref, win_ref, rgcw_ref, rgcb_ref, wri_ref, bri_ref, lam_ref,
                  wgate_ref, bgate_ref, ng_ref, hones_ref, cmask_ref, bdmask_ref,
                  cvw_ref, cvb_ref, gng_ref, gnb_ref, gones_ref,
                  wout_ref, lng_ref, lnb_ref,
                  o_ref,
                  mix_ref, hista_ref, histc_ref, h_ref, st_ref, z_ref, wout16_ref):
    tm = MIX_TM
    i = pl.program_id(0)
    tile = jnp.minimum(i, pl.num_programs(0) - 2)

    @pl.when(i == 0)
    def _():
        z_ref[...] = jnp.zeros_like(z_ref)
        wout16_ref[...] = wout_ref[...].astype(BF16)

    @pl.when(lax.rem(tile, SEQ // MIX_TM) == 0)
    def _():
        hista_ref[...] = jnp.zeros_like(hista_ref)
        histc_ref[...] = jnp.zeros_like(histc_ref)
        h_ref[...] = jnp.zeros_like(h_ref)
        st_ref[...] = jnp.zeros_like(st_ref)

    x = x_ref[...]
    xb = x.astype(BF16)

    def rg_group():
        xy = _dot(xb, win_ref[:, COL_RG:COL_RG + 2 * W_A])
        yield
        ya = xy[:, W_A:]
        xc = _causal_conv(xy[:, :W_A], hista_ref, rgcw_ref, rgcb_ref, RG_CONV_K)
        ri = _dot(xc.astype(BF16), wri_ref[...]) + bri_ref[...]
        yield
        r = _sigmoid(ri[:, :W_A])
        ig = _sigmoid(ri[:, W_A:])
        yield
        log_a = (-RG_C * r) * _softplus(-lam_ref[...])
        a = jnp.exp(log_a)
        u = jnp.sqrt(_neg_expm1_of_2log(a, log_a)) * (ig * xc)
        yield
        h = _linear_recurrence(a, u, h_ref)
        yield
        gelu = 0.5 * ya * (1.0 + jnp.tanh(0.7978845608028654 * (ya + 0.044715 * (ya * ya * ya))))
        mix_ref[:, 0:W_A] = (h * gelu).astype(BF16)

    def gla_group():
        gl = _dot(xb, win_ref[:, COL_GLA:COL_GLA + N_GLA])
        yield
        q = gl[:, 0:QK_PAD]
        kk = gl[:, QK_PAD:2 * QK_PAD]
        v = gl[:, 2 * QK_PAD:2 * QK_PAD + W_B]
        g = gl[:, 2 * QK_PAD + W_B:2 * QK_PAD + 2 * W_B]
        lr = gl[:, 2 * QK_PAD + 2 * W_B:]
        gate_pre = _dot(lr.astype(BF16), wgate_ref[...]) + bgate_ref[...]
        la = (jnp.minimum(gate_pre, 0.0) - jnp.log(1.0 + jnp.exp(-jnp.abs(gate_pre)))) * (1.0 / GLA_TAU)
        cs = [_rows(la, 0)]
        for gi in range(1, MIX_G):
            cs.append(cs[-1] + _rows(la, gi))
        sub = lax.broadcasted_iota(jnp.int32, (SUBLANES, QK_PAD), 0)
        seg_in_chunk = sub & (SEG_PER_CHUNK - 1)
        seg_tot = cs[-1]
        fix = jnp.zeros_like(seg_tot)
        for j in range(1, SEG_PER_CHUNK):
            fix = fix + jnp.where(seg_in_chunk >= j, pltpu.roll(seg_tot, j, 0), 0.0)
        bc = jnp.concatenate([c + fix for c in cs], axis=0)
        blast = cs[-1] + fix
        for j in range(1, SEG_PER_CHUNK):
            blast = jnp.where(seg_in_chunk == SEG_PER_CHUNK - 1 - j,
                              pltpu.roll(blast, SUBLANES - j, 0), blast)
        blast_t = jnp.concatenate([blast] * MIX_G, axis=0)
        decay = jnp.exp(blast)
        yield
        qd = (q * (GLA_DK ** -0.5)) * jnp.exp(bc)
        ki = kk * jnp.exp(-bc)
        kt = kk * jnp.exp(blast_t - bc)

        lane_q = lax.broadcasted_iota(jnp.int32, (SUBLANES, QK_PAD), 1)
        lane_v = lax.broadcasted_iota(jnp.int32, (SUBLANES, W_B), 1)
        head_q = [(lane_q >= hh * GLA_DK) & (lane_q < (hh + 1) * GLA_DK) for hh in range(GLA_HEADS)]
        head_v = [(lane_v >= hh * GLA_DV) & (lane_v < (hh + 1) * GLA_DV) for hh in range(GLA_HEADS)]
        in_chunk = [(sub >= c * SEG_PER_CHUNK) & (sub < (c + 1) * SEG_PER_CHUNK) for c in range(MIX_NCH)]

        def keep(val, m8):
            return jnp.concatenate([jnp.where(m8, _rows(val, gi), 0.0) for gi in range(MIX_G)], axis=0)

        ki_heads = jnp.concatenate([keep(ki, m) for m in head_q], axis=0).astype(BF16)
        yield
        s = _dot_nt(qd.astype(BF16), ki_heads)
        v_heads = jnp.concatenate([keep(v, m) for m in head_v], axis=0).astype(BF16)
        yield
        allowed = cmask_ref[...] > 0.5
        p = jnp.concatenate([jnp.where(allowed, s[:, hh * tm:(hh + 1) * tm], 0.0)
                             for hh in range(GLA_HEADS)], axis=1).astype(BF16)
        kt_chunks = jnp.concatenate([keep(kt, m) for m in in_chunk], axis=1).astype(BF16)
        yield
        o = _dot(p, v_heads)
        inc = _dot_tn(v.astype(BF16), kt_chunks)
        yield
        bd = bdmask_ref[...]
        st = st_ref[...]
        states = []
        for c in range(MIX_NCH):
            states.append(st.astype(BF16))
            row = (c + 1) * SEG_PER_CHUNK - 1
            st = st * decay[row:row + 1, :] + inc[:, c * QK_PAD:(c + 1) * QK_PAD] * bd
        st_ref[...] = st
        qd_chunks = jnp.concatenate([keep(qd, m) for m in in_chunk], axis=1).astype(BF16)
        yield
        o = o + _dot_nt(qd_chunks, jnp.concatenate(states, axis=1))
        yield
        ms = _group_sum(o * o, hones_ref[...]) * (1.0 / GLA_DV)
        yield
        o = o * lax.rsqrt(ms + EPS) * ng_ref[...]
        mix_ref[:, W_A:W_A + W_B] = (o * (g * _sigmoid(g))).astype(BF16)

    def conv_group():
        cc = _dot(xb, win_ref[:, COL_CV:COL_CV + 2 * W_C])
        yield
        cur = cc[:, :W_C] * _sigmoid(cc[:, W_C:])
        ext = jnp.concatenate(_shifted_history(cur, histc_ref, CONV_K - 1) + [cur], axis=0)
        acc = cvb_ref[...]
        for k in range(CONV_K):
            acc = acc + cvw_ref[k:k + 1, :] * _rows(ext, k, MIX_G)
            if k % CV_TAPS_PER_STAGE == CV_TAPS_PER_STAGE - 1:
                yield
        gsz = W_C // CONV_GROUPS
        mu = _group_sum(acc, gones_ref[...]) * (1.0 / gsz)
        yield
        dd = acc - mu
        var = _group_sum(dd * dd, gones_ref[...]) * (1.0 / gsz)
        yield
        un = dd * lax.rsqrt(var + EPS) * gng_ref[...] + gnb_ref[...]
        mix_ref[:, W_A + W_B:] = (un * _sigmoid(un)).astype(BF16)

    def lagged_ln():
        for st in _lagged_layer_norm(z_ref, lng_ref, lnb_ref, o_ref, tm):
            st()
            yield

    groups = {"rg": rg_group(), "gla": gla_group(), "cv": conv_group(), "ln": lagged_ln()}
    for name in MIX_STAGE_ORDER:
        next(groups[name], None)
    for gen in groups.values():
        for _ in gen:
            pass

    y = _dot(mix_ref[...], wout16_ref[...])
    z_ref[...] = ALPHA * x + y


def _mixer(x2d, p, layer):
    t = x2d.shape[0]
    tm = MIX_TM
    n_tiles = t // tm
    return pl.pallas_call(
        _mixer_kernel,
        grid=(n_tiles + 1,),
        in_specs=[
            pl.BlockSpec((tm, D_MODEL), lambda i: (jnp.minimum(i, n_tiles - 1), 0)),
            _const_spec((D_MODEL, N_IN_PAD), layer),
            _const_spec((RG_CONV_K, W_A), layer),
            _const_spec((1, W_A), layer),
            _const_spec((W_A, 2 * W_A), layer),
            _const_spec((1, 2 * W_A), layer),
            _const_spec((1, W_A), layer),
            _const_spec((LR_PAD, QK_PAD), layer),
            _const_spec((1, QK_PAD), layer),
            _const_spec((1, W_B), layer),
            _shared_spec((W_B, W_B)),
            _shared_spec((tm, tm)),
            _shared_spec((W_B, QK_PAD)),
            _const_spec((CONV_K, W_C), layer),
            _const_spec((1, W_C), layer),
            _const_spec((1, W_C), layer),
            _const_spec((1, W_C), layer),
            _shared_spec((W_C, W_C)),
            _const_spec((D_MODEL, D_MODEL), layer),
            _const_spec((1, D_MODEL), layer),
            _const_spec((1, D_MODEL), layer),
        ],
        out_specs=pl.BlockSpec((tm, D_MODEL), lambda i: (jnp.maximum(i - 1, 0), 0)),
        out_shape=jax.ShapeDtypeStruct((t, D_MODEL), F32),
        scratch_shapes=[
            pltpu.VMEM((tm, D_MODEL), BF16),
            pltpu.VMEM(((RG_CONV_K - 1) * SUBLANES, W_A), F32),
            pltpu.VMEM(((CONV_K - 1) * SUBLANES, W_C), F32),
            pltpu.VMEM((SUBLANES, W_A), F32),
            pltpu.VMEM((W_B, QK_PAD), F32),
            pltpu.VMEM((tm, D_MODEL), F32),
            pltpu.VMEM((D_MODEL, D_MODEL), BF16),
        ],
        compiler_params=pltpu.CompilerParams(
            dimension_semantics=("arbitrary",), vmem_limit_bytes=VMEM_LIMIT),
        name="mixer_ln",
    )(x2d, p["mix_w_in"], p["rg_conv_w"], p["rg_conv_b"], p["rg_w_ri"], p["rg_b_ri"],
      p["rg_lambda"], p["gla_w_gate"], p["gla_b_gate"], p["gla_norm_g"], p["head_ones"],
      p["chunk_causal"], p["state_blockdiag"],
      p["cv_dw_w"], p["cv_dw_b"], p["cv_gn_g"], p["cv_gn_b"], p["group_ones"],
      p["mix_w_out"], p["ln2_g"], p["ln2_b"])


def _kv_kernel(m_ref, w_ref, k_ref, v_ref):
    kv = _dot(m_ref[...].astype(BF16), w_ref[...].astype(BF16))
    k_ref[...] = kv[:, :D_MODEL].astype(BF16)
    v_ref[...] = kv[:, D_MODEL:].astype(BF16)


def _kv_proj(mem3d, w_kv):
    bsz, m, _ = mem3d.shape
    depth = w_kv.shape[0]
    ospec = pl.BlockSpec((None, bsz * m, D_MODEL), lambda l: (l, 0, 0))
    k, v = pl.pallas_call(
        _kv_kernel,
        grid=(depth,),
        in_specs=[pl.BlockSpec((bsz * m, D_MODEL), lambda l: (0, 0)),
                  pl.BlockSpec((None, D_MODEL, 2 * D_MODEL), lambda l: (l, 0, 0))],
        out_specs=[ospec, ospec],
        out_shape=[jax.ShapeDtypeStruct((depth, bsz * m, D_MODEL), BF16)] * 2,
        compiler_params=pltpu.CompilerParams(
            dimension_semantics=("arbitrary",), vmem_limit_bytes=VMEM_LIMIT),
        name="xattn_kv",
    )(mem3d.reshape(bsz * m, D_MODEL), w_kv)
    return k.reshape(depth, bsz, m, D_MODEL), v.reshape(depth, bsz, m, D_MODEL)


def _xattn_kernel(x_ref, k_ref, v_ref, wq_ref, wo_ref, g_ref, b_ref, o_ref, att_ref, z_ref):
    i = pl.program_id(0)
    n_tiles = pl.num_programs(0) - 1

    @pl.when(i == 0)
    def _():
        z_ref[...] = jnp.zeros_like(z_ref)

    @pl.when(i < n_tiles)
    def _():
        x = x_ref[...]
        ln_stages = _lagged_layer_norm(z_ref, g_ref, b_ref, o_ref, XA_TM)
        per_phase = len(ln_stages) // XA_HEADS
        q = _dot(x.astype(BF16), wq_ref[...].astype(BF16)).astype(BF16)
        for hh in range(XA_HEADS):
            tokens = [st() for st in ln_stages[hh * per_phase:(hh + 1) * per_phase]]
            c0 = hh * XA_HEAD_DIM
            s = _dot_nt(q[:, c0:c0 + XA_HEAD_DIM], k_ref[:, c0:c0 + XA_HEAD_DIM]) * (XA_HEAD_DIM ** -0.5)
            s = _tie(s, tokens)
            e = jnp.exp(s - jnp.max(s, axis=-1, keepdims=True))
            p = e / jnp.sum(e, axis=-1, keepdims=True)
            att_ref[:, c0:c0 + XA_HEAD_DIM] = _dot(p.astype(BF16), v_ref[:, c0:c0 + XA_HEAD_DIM]).astype(BF16)
        y = _dot(att_ref[...], wo_ref[...].astype(BF16))
        z_ref[...] = ALPHA * x + y

    @pl.when(i == n_tiles)
    def _():
        o_ref[...] = _layer_norm(z_ref[...], g_ref[...], b_ref[...])


def _xattn(x2d, k4d, v4d, w_q, w_o, g, b, layer):
    t = x2d.shape[0]
    _, bsz, m, _ = k4d.shape
    tm = XA_TM
    n_tiles = t // tm
    tiles_per_batch = n_tiles // bsz
    mspec = pl.BlockSpec((None, None, m, D_MODEL),
                         lambda i: (layer, jnp.minimum(i, n_tiles - 1) // tiles_per_batch, 0, 0))
    return pl.pallas_call(
        _xattn_kernel,
        grid=(n_tiles + 1,),
        in_specs=[
            pl.BlockSpec((tm, D_MODEL), lambda i: (jnp.minimum(i, n_tiles - 1), 0)),
            mspec, mspec,
            _const_spec((D_MODEL, D_MODEL), layer),
            _const_spec((D_MODEL, D_MODEL), layer),
            _const_spec((1, D_MODEL), layer),
            _const_spec((1, D_MODEL), layer),
        ],
        out_specs=pl.BlockSpec((tm, D_MODEL), lambda i: (jnp.maximum(i - 1, 0), 0)),
        out_shape=jax.ShapeDtypeStruct((t, D_MODEL), F32),
        scratch_shapes=[pltpu.VMEM((tm, D_MODEL), BF16), pltpu.VMEM((tm, D_MODEL), F32)],
        compiler_params=pltpu.CompilerParams(
            dimension_semantics=("arbitrary",), vmem_limit_bytes=VMEM_LIMIT),
        name="xattn_ln",
    )(x2d, k4d, v4d, w_q, w_o, g, b)


def _block_ones(n, group):
    idx = np.arange(n) // group
    return jnp.asarray((idx[:, None] == idx[None, :]).astype(np.float32), dtype=BF16)


def _chunk_causal_mask():
    row = np.arange(MIX_TM)
    step = (row % SUBLANES) * MIX_G + row // SUBLANES
    chunk = step // GLA_CHUNK
    ok = (chunk[:, None] == chunk[None, :]) & (step[:, None] >= step[None, :])
    return jnp.asarray(ok.astype(np.float32))


def _state_blockdiag_mask():
    vh = np.arange(W_B) // GLA_DV
    kh = np.arange(QK_PAD) // GLA_DK
    return jnp.asarray((vh[:, None] == kh[None, :]).astype(np.float32))


NQK = GLA_HEADS * GLA_DK
SRC_Q = 2 * W_A
SRC_K = SRC_Q + NQK
SRC_V = SRC_K + NQK
SRC_LR = SRC_V + 2 * W_B
SRC_CV = SRC_LR + GLA_RANK
N_IN_SRC = SRC_CV + 2 * W_C
RELAYOUT_ROWS = 256


def _relayout_kernel(w_ref, o_ref):
    w = w_ref[...]

    def zeros(n):
        return jnp.zeros((w.shape[0], n), w.dtype)

    o_ref[...] = jnp.concatenate([
        w[:, :SRC_Q],
        w[:, SRC_Q:SRC_K], zeros(QK_PAD - NQK),
        w[:, SRC_K:SRC_V], zeros(QK_PAD - NQK),
        w[:, SRC_V:SRC_LR],
        w[:, SRC_LR:SRC_CV], zeros(LR_PAD - GLA_RANK),
        w[:, SRC_CV:],
    ], axis=-1).astype(BF16)


def _relayout_mix_w_in(mix_w_in):
    depth = mix_w_in.shape[0]
    return pl.pallas_call(
        _relayout_kernel,
        grid=(depth, D_MODEL // RELAYOUT_ROWS),
        in_specs=[pl.BlockSpec((None, RELAYOUT_ROWS, N_IN_SRC), lambda l, r: (l, r, 0))],
        out_specs=pl.BlockSpec((None, RELAYOUT_ROWS, N_IN_PAD), lambda l, r: (l, r, 0)),
        out_shape=jax.ShapeDtypeStruct((depth, D_MODEL, N_IN_PAD), BF16),
        compiler_params=pltpu.CompilerParams(
            dimension_semantics=("arbitrary", "arbitrary"), vmem_limit_bytes=VMEM_LIMIT),
        name="mix_w_in_relayout",
    )(mix_w_in)


def _prepare_params(mix_w_in, rg_conv_b, rg_w_r, rg_b_r, rg_w_i, rg_b_i, rg_lambda,
                    gla_w_gate, gla_b_gate, gla_norm_g, cv_dw_b, cv_gn_g, cv_gn_b):
    L = DEPTH
    nqk = NQK
    w_in = _relayout_mix_w_in(mix_w_in)

    eye = jnp.eye(RG_HEADS, dtype=rg_w_r.dtype)

    def blockdiag(w):
        return jnp.einsum("lhij,hg->lhigj", w, eye).reshape(L, W_A, W_A)

    w_ri = jnp.concatenate([blockdiag(rg_w_r), blockdiag(rg_w_i)], axis=-1).astype(BF16)
    b_ri = jnp.concatenate([rg_b_r, rg_b_i], axis=-1)[:, None, :]
    w_gate = jnp.pad(gla_w_gate, ((0, 0), (0, LR_PAD - GLA_RANK), (0, QK_PAD - nqk))).astype(BF16)
    b_gate = jnp.pad(gla_b_gate, ((0, 0), (0, QK_PAD - nqk)))[:, None, :]
    return {
        "mix_w_in": w_in,
        "rg_conv_b": rg_conv_b[:, None, :],
        "rg_w_ri": w_ri,
        "rg_b_ri": b_ri,
        "rg_lambda": rg_lambda[:, None, :],
        "gla_w_gate": w_gate,
        "gla_b_gate": b_gate,
        "gla_norm_g": jnp.tile(gla_norm_g, (1, GLA_HEADS))[:, None, :],
        "head_ones": _block_ones(W_B, GLA_DV),
        "chunk_causal": _chunk_causal_mask(),
        "state_blockdiag": _state_blockdiag_mask(),
        "cv_dw_b": cv_dw_b[:, None, :],
        "cv_gn_g": cv_gn_g[:, None, :],
        "cv_gn_b": cv_gn_b[:, None, :],
        "group_ones": _block_ones(W_C, W_C // CONV_GROUPS),
    }


def kernel(x, mem, ffn1_w_in, ffn1_w_out, ln1_g, ln1_b, mix_w_in, rg_conv_w, rg_conv_b, rg_w_r, rg_b_r, rg_w_i, rg_b_i, rg_lambda, gla_w_gate, gla_b_gate, gla_norm_g, cv_dw_w, cv_dw_b, cv_gn_g, cv_gn_b, mix_w_out, ln2_g, ln2_b, xa_w_q, xa_w_kv, xa_w_o, ln3_g, ln3_b, ffn2_w_in, ffn2_w_out, ln4_g, ln4_b):
    bsz, s, dm = x.shape
    p = _prepare_params(mix_w_in, rg_conv_b, rg_w_r, rg_b_r, rg_w_i, rg_b_i, rg_lambda,
                        gla_w_gate, gla_b_gate, gla_norm_g, cv_dw_b, cv_gn_g, cv_gn_b)
    p.update({
        "rg_conv_w": rg_conv_w, "cv_dw_w": cv_dw_w,
        "mix_w_out": mix_w_out,
        "ln2_g": ln2_g[:, None, :], "ln2_b": ln2_b[:, None, :],
    })
    ln1g, ln1b = ln1_g[:, None, :], ln1_b[:, None, :]
    ln3g, ln3b = ln3_g[:, None, :], ln3_b[:, None, :]
    ln4g, ln4b = ln4_g[:, None, :], ln4_b[:, None, :]

    nt = s // MIX_TM
    x = x.reshape(bsz, nt, SUBLANES, MIX_G, dm).swapaxes(2, 3).reshape(bsz, s, dm)
    x = x.reshape(bsz * s, dm)
    k4d, v4d = _kv_proj(mem, xa_w_kv)
    for l in range(DEPTH):
        x = _ffn(x, ffn1_w_in, ffn1_w_out, ln1g, ln1b, l)
        x = _mixer(x, p, l)
        x = _xattn(x, k4d, v4d, xa_w_q, xa_w_o, ln3g, ln3b, l)
        x = _ffn(x, ffn2_w_in, ffn2_w_out, ln4g, ln4b, l)
    return x.reshape(bsz, nt, MIX_G, SUBLANES, dm).swapaxes(2, 3).reshape(bsz, s, dm)
```
